```python
import jax, jax.numpy as jnp
from jax import lax
import numpy as np

D_MODEL = 4096
BATCH = 2
SEQ = 4096
DEPTH = 1

CHUNK = 64
LEFT_CHUNKS = 8
BAND = LEFT_CHUNKS + 1
HEAD_DIM = 128
N_HEADS_A = 16
N_HEADS_B = 16
WIDTH_A = N_HEADS_A * HEAD_DIM
WIDTH_B = N_HEADS_B * HEAD_DIM
MAX_REL = 256
N_REL = 2 * MAX_REL + 1
Q_BLOCK = 128
N_BRANCHES = 2
D_FF = -(-(8 * D_MODEL) // (3 * 256)) * 256
D_IN = 3 * WIDTH_A + 3 * WIDTH_B + N_HEADS_B + N_BRANCHES * D_MODEL
RMS_EPS = 1e-6
NEG_INF = -1e30

kernel_name = "hybrid_chunked_relpos_fox_gated_block"


def rms_norm(x, g):
    xf = x.astype(jnp.float32)
    y = xf * lax.rsqrt(jnp.mean(xf * xf, axis=-1, keepdims=True) + RMS_EPS)
    return (y * g.astype(jnp.float32)).astype(x.dtype)


def chunked_relpos_attention(q, k, v, rel_bias):
    b, s, h, dh = q.shape
    n_chunks = s // CHUNK
    q = q.reshape(b, n_chunks, CHUNK, h, dh)
    pad = ((0, 0), (LEFT_CHUNKS, 0), (0, 0), (0, 0), (0, 0))
    kp = jnp.pad(k.reshape(b, n_chunks, CHUNK, h, dh), pad)
    vp = jnp.pad(v.reshape(b, n_chunks, CHUNK, h, dh), pad)
    band_idx = jnp.arange(n_chunks)[:, None] + jnp.arange(BAND)[None, :]
    k_band = kp[:, band_idx].reshape(b, n_chunks, BAND * CHUNK, h, dh)
    v_band = vp[:, band_idx].reshape(b, n_chunks, BAND * CHUNK, h, dh)
    valid = (band_idx - LEFT_CHUNKS) >= 0
    valid = jnp.repeat(valid, CHUNK, axis=1)
    a_pos = jnp.arange(CHUNK)[:, None, None]
    slot = jnp.arange(BAND)[None, :, None]
    b_pos = jnp.arange(CHUNK)[None, None, :]
    dist = ((LEFT_CHUNKS - slot) * CHUNK + a_pos - b_pos).reshape(CHUNK, BAND * CHUNK)
    rel_idx = jnp.clip(dist, -MAX_REL, MAX_REL) + MAX_REL
    bias = rel_bias.astype(jnp.float32)[:, rel_idx]
    scale = HEAD_DIM ** -0.5
    logits = jnp.einsum('bcqhd,bckhd->bhcqk', q, k_band,
                        preferred_element_type=jnp.float32) * scale
    logits = logits + bias[None, :, None, :, :]
    logits = jnp.where(valid[None, None, :, None, :], logits, NEG_INF)
    p = jax.nn.softmax(logits, axis=-1).astype(v.dtype)
    out = jnp.einsum('bhcqk,bckhd->bcqhd', p, v_band)
    return out.reshape(b, s, h * dh)


def forgetting_attention(q, k, v, log_f):
    b, s, h, dh = q.shape
    n_blocks = s // Q_BLOCK
    cum = jnp.cumsum(log_f, axis=1)
    cum_k = cum.transpose(0, 2, 1)
    q_blocks = q.reshape(b, n_blocks, Q_BLOCK, h, dh).transpose(1, 0, 2, 3, 4)
    c_blocks = cum.reshape(b, n_blocks, Q_BLOCK, h).transpose(1, 0, 3, 2)
    starts = jnp.arange(n_blocks) * Q_BLOCK
    k_pos = jnp.arange(s)
    scale = HEAD_DIM ** -0.5

    def one_block(args):
        q_i, c_i, start = args
        logits = jnp.einsum('bqhd,bkhd->bhqk', q_i, k,
                            preferred_element_type=jnp.float32) * scale
        logits = logits + c_i[..., :, None] - cum_k[:, :, None, :]
        q_pos = start + jnp.arange(Q_BLOCK)
        causal = k_pos[None, :] <= q_pos[:, None]
        logits = jnp.where(causal[None, None], logits, NEG_INF)
        p = jax.nn.softmax(logits, axis=-1).astype(v.dtype)
        return jnp.einsum('bhqk,bkhd->bqhd', p, v)

    out = lax.map(one_block, (q_blocks, c_blocks, starts))
    return out.transpose(1, 0, 2, 3, 4).reshape(b, s, h * dh)


def setup_inputs(seed: int = 0) -> dict:
    key = jax.random.key(seed)
    ks = jax.random.split(key, 16)
    L = DEPTH

    def dense(k, shape, fan_in):
        return jax.random.normal(k, shape, jnp.float32) * fan_in ** -0.5

    def normal(k, shape):
        return jax.random.normal(k, shape, jnp.float32)

    return {
        "x": normal(ks[0], (BATCH, SEQ, D_MODEL)),
        "g_mix": 1.0 + 0.02 * normal(ks[1], (L, D_MODEL)),
        "w_in": dense(ks[2], (L, D_MODEL, D_IN), D_MODEL),
        "b_f": 2.0 + 0.5 * normal(ks[3], (L, N_HEADS_B)),
        "b_gate": 0.01 * normal(ks[4], (L, N_BRANCHES * D_MODEL)),
        "rel_bias": 0.1 * normal(ks[5], (L, N_HEADS_A, N_REL)),
        "w_branch_a": dense(ks[6], (L, WIDTH_A, D_MODEL), WIDTH_A),
        "w_branch_b": dense(ks[7], (L, WIDTH_B, D_MODEL), WIDTH_B),
        "w_out": dense(ks[8], (L, D_MODEL, D_MODEL), D_MODEL),
        "g_ffn": 1.0 + 0.02 * normal(ks[9], (L, D_MODEL)),
        "w_gate_ffn": dense(ks[10], (L, D_MODEL, D_FF), D_MODEL),
        "w_up_ffn": dense(ks[11], (L, D_MODEL, D_FF), D_MODEL),
        "w_down_ffn": dense(ks[12], (L, D_FF, D_MODEL), D_FF),
        "g_final": 1.0 + 0.02 * normal(ks[13], (D_MODEL,)),
    }


def reference(x, g_mix, w_in, b_f, b_gate, rel_bias, w_branch_a, w_branch_b, w_out,
              g_ffn, w_gate_ffn, w_up_ffn, w_down_ffn, g_final):
    b, s, _ = x.shape
    sizes = [WIDTH_A, WIDTH_A, WIDTH_A, WIDTH_B, WIDTH_B, WIDTH_B, N_HEADS_B, D_MODEL, D_MODEL]
    split_at = [int(v) for v in np.cumsum(sizes)[:-1]]
    for l in range(DEPTH):
        h = rms_norm(x, g_mix[l])
        proj = jnp.einsum('bsd,de->bse', h, w_in[l])
        qa, ka, va, qb, kb, vb, f_logit, gate_a, gate_b = jnp.split(proj, split_at, axis=-1)
        heads_a = lambda t: t.reshape(b, s, N_HEADS_A, HEAD_DIM)
        heads_b = lambda t: t.reshape(b, s, N_HEADS_B, HEAD_DIM)
        o_a = chunked_relpos_attention(heads_a(qa), heads_a(ka), heads_a(va), rel_bias[l])
        log_f = jax.nn.log_sigmoid((f_logit + b_f[l]).astype(jnp.float32))
        o_b = forgetting_attention(heads_b(qb), heads_b(kb), heads_b(vb), log_f)
        u_a = jnp.einsum('bse,ed->bsd', o_a, w_branch_a[l])
        u_b = jnp.einsum('bse,ed->bsd', o_b, w_branch_b[l])
        merged = (jax.nn.sigmoid(gate_a + b_gate[l, :D_MODEL]) * u_a
                  + jax.nn.sigmoid(gate_b + b_gate[l, D_MODEL:]) * u_b)
        x = x + jnp.einsum('bsd,de->bse', merged, w_out[l])
        h2 = rms_norm(x, g_ffn[l])
        hidden = jax.nn.silu(jnp.einsum('bsd,df->bsf', h2, w_gate_ffn[l])) * \
            jnp.einsum('bsd,df->bsf', h2, w_up_ffn[l])
        x = x + jnp.einsum('bsf,fd->bsd', hidden, w_down_ffn[l])
    return rms_norm(x, g_final)
```

```python
import functools

import jax
import jax.numpy as jnp
from jax import lax
from jax.experimental import pallas as pl
from jax.experimental.pallas import tpu as pltpu

F32 = jnp.float32
BF16 = jnp.bfloat16

CHUNK = 64
LEFT_CHUNKS = 8
HEAD_DIM = 128
N_HEADS = 16
WIDTH = N_HEADS * HEAD_DIM
MAX_REL = 256
RMS_EPS = 1e-6
NEG_INF = -1e30
SCALE = HEAD_DIM ** -0.5

LANES = 128
VMEM_LIMIT = 56 * 1024 * 1024

TQ_A = 256
KB_A = 3 * TQ_A
T_B = 256


def _params(*sem):
    return pltpu.CompilerParams(dimension_semantics=sem, vmem_limit_bytes=VMEM_LIMIT)


def _rmsnorm_kernel(x_ref, g_ref, o_ref):
    x = x_ref[...]
    y = x * lax.rsqrt(jnp.mean(x * x, axis=-1, keepdims=True) + RMS_EPS)
    o_ref[...] = (y * g_ref[...]).astype(o_ref.dtype)


def _rmsnorm(x, g, out_dtype, tm=256):
    m, d = x.shape
    return pl.pallas_call(
        _rmsnorm_kernel,
        grid=(m // tm,),
        in_specs=[pl.BlockSpec((tm, d), lambda i: (i, 0)),
                  pl.BlockSpec((1, d), lambda i: (0, 0))],
        out_specs=pl.BlockSpec((tm, d), lambda i: (i, 0)),
        out_shape=jax.ShapeDtypeStruct((m, d), out_dtype),
        compiler_params=_params("parallel"),
        name="rmsnorm",
    )(x, g.reshape(1, d))


def _mm_kernel(a_ref, b_ref, o_ref):
    o_ref[...] = jnp.dot(a_ref[...], b_ref[...],
                         preferred_element_type=F32).astype(o_ref.dtype)


def _matmul(a, b, out_dtype, tm, tn, name):
    m, k = a.shape
    _, n = b.shape
    return pl.pallas_call(
        _mm_kernel,
        grid=(m // tm, n // tn),
        in_specs=[pl.BlockSpec((tm, k), lambda i, j: (i, 0)),
                  pl.BlockSpec((k, tn), lambda i, j: (0, j))],
        out_specs=pl.BlockSpec((tm, tn), lambda i, j: (i, j)),
        out_shape=jax.ShapeDtypeStruct((m, n), out_dtype),
        compiler_params=_params("parallel", "parallel"),
        name=name,
    )(a, b)


def _mm_res_kernel(a_ref, b_ref, r_ref, o_ref):
    o_ref[...] = r_ref[...] + jnp.dot(a_ref[...], b_ref[...], preferred_element_type=F32)


def _matmul_residual(a, b, r, tm, tn, name):
    m, k = a.shape
    _, n = b.shape
    return pl.pallas_call(
        _mm_res_kernel,
        grid=(m // tm, n // tn),
        in_specs=[pl.BlockSpec((tm, k), lambda i, j: (i, 0)),
                  pl.BlockSpec((k, tn), lambda i, j: (0, j)),
                  pl.BlockSpec((tm, tn), lambda i, j: (i, j))],
        out_specs=pl.BlockSpec((tm, tn), lambda i, j: (i, j)),
        out_shape=jax.ShapeDtypeStruct((m, n), F32),
        compiler_params=_params("parallel", "parallel"),
        name=name,
    )(a, b, r)


def _merge_kernel(oa_ref, ob_ref, wa_ref, wb_ref, ga_ref, gb_ref, ba_ref, bb_ref, o_ref):
    ua = jnp.dot(oa_ref[...], wa_ref[...], preferred_element_type=F32)
    ub = jnp.dot(ob_ref[...], wb_ref[...], preferred_element_type=F32)
    merged = (jax.nn.sigmoid(ga_ref[...] + ba_ref[...]) * ua
              + jax.nn.sigmoid(gb_ref[...] + bb_ref[...]) * ub)
    o_ref[...] = merged.astype(o_ref.dtype)


def _merge(o_a, o_b, w_a, w_b, gates, b_gate, tm=1024, tn=512):
    m, k = o_a.shape
    d = w_a.shape[1]
    nb = d // tn
    bg = b_gate.reshape(1, 2 * d)
    return pl.pallas_call(
        _merge_kernel,
        grid=(m // tm, nb),
        in_specs=[pl.BlockSpec((tm, k), lambda i, j: (i, 0)),
                  pl.BlockSpec((tm, k), lambda i, j: (i, 0)),
                  pl.BlockSpec((k, tn), lambda i, j: (0, j)),
                  pl.BlockSpec((k, tn), lambda i, j: (0, j)),
                  pl.BlockSpec((tm, tn), lambda i, j: (i, j)),
                  pl.BlockSpec((tm, tn), lambda i, j: (i, j + nb)),
                  pl.BlockSpec((1, tn), lambda i, j: (0, j)),
                  pl.BlockSpec((1, tn), lambda i, j: (0, j + nb))],
        out_specs=pl.BlockSpec((tm, tn), lambda i, j: (i, j)),
        out_shape=jax.ShapeDtypeStruct((m, d), BF16),
        compiler_params=_params("parallel", "parallel"),
        name="merge",
    )(o_a, o_b, w_a, w_b, gates, gates, bg, bg)


def _swiglu_up_kernel(a_ref, wg_ref, wu_ref, o_ref):
    a = a_ref[...]
    g = jnp.dot(a, wg_ref[...], preferred_element_type=F32)
    u = jnp.dot(a, wu_ref[...], preferred_element_type=F32)
    o_ref[...] = (g * jax.nn.sigmoid(g) * u).astype(o_ref.dtype)


def _swiglu_up(a, wg, wu, tm=1024, tn=256):
    m, k = a.shape
    f = wg.shape[1]
    return pl.pallas_call(
        _swiglu_up_kernel,
        grid=(m // tm, f // tn),
        in_specs=[pl.BlockSpec((tm, k), lambda i, j: (i, 0)),
                  pl.BlockSpec((k, tn), lambda i, j: (0, j)),
                  pl.BlockSpec((k, tn), lambda i, j: (0, j))],
        out_specs=pl.BlockSpec((tm, tn), lambda i, j: (i, j)),
        out_shape=jax.ShapeDtypeStruct((m, f), BF16),
        compiler_params=_params("parallel", "parallel"),
        name="swiglu_up",
    )(a, wg, wu)


def _cum_kernel(f_ref, b_ref, c_ref, ct_ref):
    x = f_ref[...] + b_ref[...]
    c = jnp.minimum(x, 0.0) - jnp.log1p(jnp.exp(-jnp.abs(x)))
    n = c.shape[0]
    row = lax.broadcasted_iota(jnp.int32, c.shape, 0)
    shift = 1
    while shift < n:
        c = c + jnp.where(row >= shift, pltpu.roll(c, shift, axis=0), 0.0)
        shift *= 2
    c_ref[...] = c
    ct_ref[...] = c.T


def _forget_scan(f_pad, b_pad, batch, seq):
    return pl.pallas_call(
        _cum_kernel,
        grid=(batch,),
        in_specs=[pl.BlockSpec((seq, LANES), lambda b: (b, 0)),
                  pl.BlockSpec((1, LANES), lambda b: (0, 0))],
        out_specs=[pl.BlockSpec((seq, LANES), lambda b: (b, 0)),
                   pl.BlockSpec((None, LANES, seq), lambda b: (b, 0, 0))],
        out_shape=[jax.ShapeDtypeStruct((batch * seq, LANES), F32),
                   jax.ShapeDtypeStruct((batch, LANES, seq), F32)],
        compiler_params=_params("parallel"),
        name="forget_scan",
    )(f_pad, b_pad)


def _attn_a_kernel(q_ref, k0_ref, k1_ref, k2_ref, v0_ref, v1_ref, v2_ref, bias_ref, o_ref):
    qi = pl.program_id(2)
    k = jnp.concatenate([k0_ref[...], k1_ref[...], k2_ref[...]], axis=0)
    v = jnp.concatenate([v0_ref[...], v1_ref[...], v2_ref[...]], axis=0)
    s = lax.dot_general(q_ref[...], k, (((1,), (1,)), ((), ())),
                        preferred_element_type=F32) * SCALE
    s = s + bias_ref[...]
    qc = lax.broadcasted_iota(jnp.int32, s.shape, 0) // CHUNK
    col = lax.broadcasted_iota(jnp.int32, s.shape, 1)
    kc = col // CHUNK
    valid = (kc >= qc) & (kc <= qc + LEFT_CHUNKS) & (col + (qi - 2) * TQ_A >= 0)
    s = jnp.where(valid, s, NEG_INF)
    m = jnp.max(s, axis=-1, keepdims=True)
    p = jnp.exp(s - m)
    l = jnp.sum(p, axis=-1, keepdims=True)
    o = jnp.dot(p.astype(BF16), v, preferred_element_type=F32)
    o_ref[...] = (o / l).astype(o_ref.dtype)


def _attn_a(qkv, bias_tab, batch, seq):
    nq = seq // TQ_A
    h = N_HEADS

    def kv_spec(group, back):
        return pl.BlockSpec(
            (TQ_A, HEAD_DIM),
            lambda b, hh, qi: (b * nq + jnp.maximum(qi - back, 0), group * h + hh))

    return pl.pallas_call(
        _attn_a_kernel,
        grid=(batch, h, nq),
        in_specs=[pl.BlockSpec((TQ_A, HEAD_DIM), lambda b, hh, qi: (b * nq + qi, hh)),
                  kv_spec(1, 2), kv_spec(1, 1), kv_spec(1, 0),
                  kv_spec(2, 2), kv_spec(2, 1), kv_spec(2, 0),
                  pl.BlockSpec((None, TQ_A, KB_A), lambda b, hh, qi: (hh, 0, 0))],
        out_specs=pl.BlockSpec((TQ_A, HEAD_DIM), lambda b, hh, qi: (b * nq + qi, hh)),
        out_shape=jax.ShapeDtypeStruct((batch * seq, WIDTH), BF16),
        compiler_params=_params("parallel", "parallel", "parallel"),
        name="attn_a",
    )(qkv, qkv, qkv, qkv, qkv, qkv, qkv, bias_tab)


def _fox_kernel(q_ref, k_ref, v_ref, c_ref, ct_ref, o_ref):
    hh = pl.program_id(1)
    qi = pl.program_id(2)
    q = q_ref[...]
    lane = lax.broadcasted_iota(jnp.int32, c_ref.shape, 1)
    cq = jnp.sum(jnp.where(lane == hh, c_ref[...], 0.0), axis=-1, keepdims=True)

    def step(j, carry, masked):
        m, l, acc = carry
        start = pl.multiple_of(j * T_B, T_B)
        k = k_ref[pl.ds(start, T_B), :]
        v = v_ref[pl.ds(start, T_B), :]
        s = lax.dot_general(q, k, (((1,), (1,)), ((), ())),
                            preferred_element_type=F32) * SCALE
        s = s + cq - ct_ref[:, pl.ds(start, T_B)]
        if masked:
            r = lax.broadcasted_iota(jnp.int32, s.shape, 0)
            c = lax.broadcasted_iota(jnp.int32, s.shape, 1)
            s = jnp.where(c <= r, s, NEG_INF)
        m_new = jnp.maximum(m, jnp.max(s, axis=-1, keepdims=True))
        alpha = jnp.exp(m - m_new)
        p = jnp.exp(s - m_new)
        l = alpha * l + jnp.sum(p, axis=-1, keepdims=True)
        acc = alpha * acc + jnp.dot(p.astype(BF16), v, preferred_element_type=F32)
        return m_new, l, acc

    init = (jnp.full((T_B, 1), NEG_INF, F32), jnp.zeros((T_B, 1), F32),
            jnp.zeros((T_B, HEAD_DIM), F32))
    carry = lax.fori_loop(0, qi, functools.partial(step, masked=False), init)
    _, l, acc = step(qi, carry, masked=True)
    o_ref[...] = (acc / l).astype(o_ref.dtype)


def _fox(qkv, c, ct, batch, seq):
    nq = seq // T_B
    h = N_HEADS
    return pl.pallas_call(
        _fox_kernel,
        grid=(batch, h, nq),
        in_specs=[pl.BlockSpec((T_B, HEAD_DIM), lambda b, hh, qi: (b * nq + qi, 3 * h + hh)),
                  pl.BlockSpec((seq, HEAD_DIM), lambda b, hh, qi: (b, 4 * h + hh)),
                  pl.BlockSpec((seq, HEAD_DIM), lambda b, hh, qi: (b, 5 * h + hh)),
                  pl.BlockSpec((T_B, LANES), lambda b, hh, qi: (b * nq + qi, 0)),
                  pl.BlockSpec((None, 1, seq), lambda b, hh, qi: (b * h + hh, 0, 0))],
        out_specs=pl.BlockSpec((T_B, HEAD_DIM), lambda b, hh, qi: (b * nq + qi, hh)),
        out_shape=jax.ShapeDtypeStruct((batch * seq, WIDTH), BF16),
        compiler_params=_params("parallel", "parallel", "parallel"),
        name="fox",
    )(qkv, qkv, qkv, c, ct)


def _relpos_bias_table(rel_bias):
    a = jnp.arange(TQ_A)[:, None]
    b = jnp.arange(KB_A)[None, :]
    idx = jnp.clip(a - b + 2 * TQ_A, -MAX_REL, MAX_REL) + MAX_REL
    return rel_bias.astype(F32)[:, idx]


def kernel(x, g_mix, w_in, b_f, b_gate, rel_bias, w_branch_a, w_branch_b, w_out,
           g_ffn, w_gate_ffn, w_up_ffn, w_down_ffn, g_final):
    batch, seq, d = x.shape
    m = batch * seq
    depth = g_mix.shape[0]
    qkv_w = 6 * WIDTH
    x2 = x.reshape(m, d)
    for l in range(depth):
        w_qkv = w_in[l, :, :qkv_w].astype(BF16)
        w_f = jnp.pad(w_in[l, :, qkv_w:qkv_w + N_HEADS], ((0, 0), (0, LANES - N_HEADS))).astype(BF16)
        w_g = w_in[l, :, qkv_w + N_HEADS:].astype(BF16)
        b_f_pad = jnp.pad(b_f[l], (0, LANES - N_HEADS)).reshape(1, LANES)

        h = _rmsnorm(x2, g_mix[l], BF16)
        qkv = _matmul(h, w_qkv, BF16, 1024, 1024, "proj_qkv")
        f_pad = _matmul(h, w_f, F32, 1024, LANES, "proj_forget")
        gates = _matmul(h, w_g, F32, 1024, 1024, "proj_gates")

        c, ct = _forget_scan(f_pad, b_f_pad, batch, seq)
        ct = ct[:, :N_HEADS, :].reshape(batch * N_HEADS, 1, seq)

        o_a = _attn_a(qkv, _relpos_bias_table(rel_bias[l]), batch, seq)
        o_b = _fox(qkv, c, ct, batch, seq)

        merged = _merge(o_a, o_b, w_branch_a[l].astype(BF16), w_branch_b[l].astype(BF16),
                        gates, b_gate[l])
        x2 = _matmul_residual(merged, w_out[l].astype(BF16), x2, 1024, 1024, "out_proj")

        h2 = _rmsnorm(x2, g_ffn[l], BF16)
        hidden = _swiglu_up(h2, w_gate_ffn[l].astype(BF16), w_up_ffn[l].astype(BF16))
        x2 = _matmul_residual(hidden, w_down_ffn[l].astype(BF16), x2, 512, 512, "ffn_down")
    return _rmsnorm(x2, g_final, F32).reshape(batch, seq, d)
```

```python
import functools
import math

import jax
import jax.numpy as jnp
from jax import lax
from jax.experimental import pallas as pl
from jax.experimental.pallas import tpu as pltpu

F32 = jnp.float32
BF16 = jnp.bfloat16

CHUNK = 64
LEFT_CHUNKS = 8
HEAD_DIM = 128
N_HEADS = 16
WIDTH = N_HEADS * HEAD_DIM
MAX_REL = 256
RMS_EPS = 1e-6
NEG_INF = -1e30
SCALE = HEAD_DIM ** -0.5
LOG2E = math.log2(math.e)

LANES = 128
VMEM_LIMIT = 56 * 1024 * 1024

HEADS_PER_STEP = 2
TQ_A = 256
KB_A = 3 * TQ_A
E_LEN = 4 * TQ_A
T_B = 512


def _params(*sem):
    return pltpu.CompilerParams(dimension_semantics=sem, vmem_limit_bytes=VMEM_LIMIT)


def _rmsnorm_kernel(x_ref, g_ref, o_ref):
    x = x_ref[...]
    y = x * lax.rsqrt(jnp.mean(x * x, axis=-1, keepdims=True) + RMS_EPS)
    o_ref[...] = (y * g_ref[...]).astype(o_ref.dtype)


def _rmsnorm(x, g, out_dtype, tm=256):
    m, d = x.shape
    return pl.pallas_call(
        _rmsnorm_kernel,
        grid=(m // tm,),
        in_specs=[pl.BlockSpec((tm, d), lambda i: (i, 0)),
                  pl.BlockSpec((1, d), lambda i: (0, 0))],
        out_specs=pl.BlockSpec((tm, d), lambda i: (i, 0)),
        out_shape=jax.ShapeDtypeStruct((m, d), out_dtype),
        compiler_params=_params("parallel"),
        name="rmsnorm",
    )(x, g.reshape(1, d))


def _cast_weights(pairs):
    @pl.when(pl.program_id(1) == 0)
    def _():
        for w_ref, wq_ref in pairs:
            wq_ref[...] = w_ref[...].astype(BF16)


def _mm_kernel(a_ref, b_ref, o_ref, bq_ref):
    _cast_weights([(b_ref, bq_ref)])
    o_ref[...] = jnp.dot(a_ref[...], bq_ref[...],
                         preferred_element_type=F32).astype(o_ref.dtype)


def _matmul(a, w, col0, n, out_dtype, tm, tn, name):
    m, k = a.shape
    cb = col0 // tn
    return pl.pallas_call(
        _mm_kernel,
        grid=(n // tn, m // tm),
        in_specs=[pl.BlockSpec((tm, k), lambda j, i: (i, 0)),
                  pl.BlockSpec((k, tn), lambda j, i: (0, j + cb))],
        out_specs=pl.BlockSpec((tm, tn), lambda j, i: (i, j)),
        out_shape=jax.ShapeDtypeStruct((m, n), out_dtype),
        scratch_shapes=[pltpu.VMEM((k, tn), BF16)],
        compiler_params=_params("arbitrary", "arbitrary"),
        name=name,
    )(a, w)


def _mm_res_kernel(a_ref, b_ref, r_ref, o_ref, bq_ref):
    _cast_weights([(b_ref, bq_ref)])
    o_ref[...] = r_ref[...] + jnp.dot(a_ref[...], bq_ref[...], preferred_element_type=F32)


def _matmul_residual(a, w, r, tm, tn, name):
    m, k = a.shape
    n = w.shape[1]
    return pl.pallas_call(
        _mm_res_kernel,
        grid=(n // tn, m // tm),
        in_specs=[pl.BlockSpec((tm, k), lambda j, i: (i, 0)),
                  pl.BlockSpec((k, tn), lambda j, i: (0, j)),
                  pl.BlockSpec((tm, tn), lambda j, i: (i, j))],
        out_specs=pl.BlockSpec((tm, tn), lambda j, i: (i, j)),
        out_shape=jax.ShapeDtypeStruct((m, n), F32),
        scratch_shapes=[pltpu.VMEM((k, tn), BF16)],
        compiler_params=_params("arbitrary", "arbitrary"),
        name=name,
    )(a, w, r)


def _mm_res_bf16_kernel(a_ref, b_ref, r_ref, o_ref):
    o_ref[...] = r_ref[...] + jnp.dot(a_ref[...], b_ref[...], preferred_element_type=F32)


def _matmul_residual_bf16(a, b, r, tm, tn, name):
    m, k = a.shape
    n = b.shape[1]
    return pl.pallas_call(
        _mm_res_bf16_kernel,
        grid=(m // tm, n // tn),
        in_specs=[pl.BlockSpec((tm, k), lambda i, j: (i, 0)),
                  pl.BlockSpec((k, tn), lambda i, j: (0, j)),
                  pl.BlockSpec((tm, tn), lambda i, j: (i, j))],
        out_specs=pl.BlockSpec((tm, tn), lambda i, j: (i, j)),
        out_shape=jax.ShapeDtypeStruct((m, n), F32),
        compiler_params=_params("parallel", "parallel"),
        name=name,
    )(a, b, r)


def _merge_kernel(oa_ref, ob_ref, wa_ref, wb_ref, ga_ref, gb_ref, ba_ref, bb_ref, o_ref,
                  waq_ref, wbq_ref):
    _cast_weights([(wa_ref, waq_ref), (wb_ref, wbq_ref)])
    ua = jnp.dot(oa_ref[...], waq_ref[...], preferred_element_type=F32)
    ub = jnp.dot(ob_ref[...], wbq_ref[...], preferred_element_type=F32)
    merged = (jax.nn.sigmoid(ga_ref[...] + ba_ref[...]) * ua
              + jax.nn.sigmoid(gb_ref[...] + bb_ref[...]) * ub)
    o_ref[...] = merged.astype(o_ref.dtype)


def _merge(o_a, o_b, w_a, w_b, gates, b_gate, tm=1024, tn=512):
    m, k = o_a.shape
    d = w_a.shape[1]
    nb = d // tn
    bg = b_gate.reshape(1, 2 * d)
    return pl.pallas_call(
        _merge_kernel,
        grid=(nb, m // tm),
        in_specs=[pl.BlockSpec((tm, k), lambda j, i: (i, 0)),
                  pl.BlockSpec((tm, k), lambda j, i: (i, 0)),
                  pl.BlockSpec((k, tn), lambda j, i: (0, j)),
                  pl.BlockSpec((k, tn), lambda j, i: (0, j)),
                  pl.BlockSpec((tm, tn), lambda j, i: (i, j)),
                  pl.BlockSpec((tm, tn), lambda j, i: (i, j + nb)),
                  pl.BlockSpec((1, tn), lambda j, i: (0, j)),
                  pl.BlockSpec((1, tn), lambda j, i: (0, j + nb))],
        out_specs=pl.BlockSpec((tm, tn), lambda j, i: (i, j)),
        out_shape=jax.ShapeDtypeStruct((m, d), BF16),
        scratch_shapes=[pltpu.VMEM((k, tn), BF16), pltpu.VMEM((k, tn), BF16)],
        compiler_params=_params("arbitrary", "arbitrary"),
        name="merge",
    )(o_a, o_b, w_a, w_b, gates, gates, bg, bg)


def _swiglu_up_kernel(a_ref, wg_ref, wu_ref, o_ref, wgq_ref, wuq_ref):
    _cast_weights([(wg_ref, wgq_ref), (wu_ref, wuq_ref)])
    a = a_ref[...]
    g = jnp.dot(a, wgq_ref[...], preferred_element_type=F32)
    u = jnp.dot(a, wuq_ref[...], preferred_element_type=F32)
    o_ref[...] = (g * jax.nn.sigmoid(g) * u).astype(o_ref.dtype)


def _swiglu_up(a, wg, wu, tm=1024, tn=256):
    m, k = a.shape
    f = wg.shape[1]
    return pl.pallas_call(
        _swiglu_up_kernel,
        grid=(f // tn, m // tm),
        in_specs=[pl.BlockSpec((tm, k), lambda j, i: (i, 0)),
                  pl.BlockSpec((k, tn), lambda j, i: (0, j)),
                  pl.BlockSpec((k, tn), lambda j, i: (0, j))],
        out_specs=pl.BlockSpec((tm, tn), lambda j, i: (i, j)),
        out_shape=jax.ShapeDtypeStruct((m, f), BF16),
        scratch_shapes=[pltpu.VMEM((k, tn), BF16), pltpu.VMEM((k, tn), BF16)],
        compiler_params=_params("arbitrary", "arbitrary"),
        name="swiglu_up",
    )(a, wg, wu)


def _cum_kernel(f_ref, b_ref, c_ref, ct_ref):
    x = f_ref[...] + b_ref[...]
    c = jnp.minimum(x, 0.0) - jnp.log1p(jnp.exp(-jnp.abs(x)))
    n = c.shape[0]
    row = lax.broadcasted_iota(jnp.int32, c.shape, 0)
    shift = 1
    while shift < n:
        c = c + jnp.where(row >= shift, pltpu.roll(c, shift, axis=0), 0.0)
        shift *= 2
    c = c * LOG2E
    c_ref[...] = c
    ct_ref[...] = c.T


def _forget_scan(f_pad, b_pad, batch, seq):
    return pl.pallas_call(
        _cum_kernel,
        grid=(batch,),
        in_specs=[pl.BlockSpec((seq, LANES), lambda b: (b, 0)),
                  pl.BlockSpec((1, LANES), lambda b: (0, 0))],
        out_specs=[pl.BlockSpec((seq, LANES), lambda b: (b, 0)),
                   pl.BlockSpec((None, LANES, seq), lambda b: (b, 0, 0))],
        out_shape=[jax.ShapeDtypeStruct((batch * seq, LANES), F32),
                   jax.ShapeDtypeStruct((batch, LANES, seq), F32)],
        compiler_params=_params("parallel"),
        name="forget_scan",
    )(f_pad, b_pad)


def _attn_a_kernel(e_ref, q_ref, k_ref, v_ref, o_ref, bias_ref):
    qi = pl.program_id(2)

    @pl.when(qi == 0)
    def _():
        qc = lax.broadcasted_iota(jnp.int32, (TQ_A, KB_A), 0) // CHUNK
        kc = lax.broadcasted_iota(jnp.int32, (TQ_A, KB_A), 1) // CHUNK
        valid = (kc >= qc) & (kc <= qc + LEFT_CHUNKS)
        for u in range(HEADS_PER_STEP):
            e = jnp.broadcast_to(e_ref[u], (TQ_A, E_LEN))
            t = pltpu.roll(e, 0, 1, stride=1, stride_axis=0)[:, E_LEN - KB_A:]
            bias_ref[u] = jnp.where(valid, t, NEG_INF) * LOG2E

    starts = [pl.multiple_of(jnp.maximum(qi - back, 0) * TQ_A, TQ_A) for back in (2, 1, 0)]
    pens = [jnp.where(qi >= 2, 0.0, NEG_INF), jnp.where(qi >= 1, 0.0, NEG_INF), None]
    for u in range(HEADS_PER_STEP):
        cols = slice(u * HEAD_DIM, (u + 1) * HEAD_DIM)
        q = q_ref[:, cols]
        parts = []
        for blk in range(3):
            k = k_ref[pl.ds(starts[blk], TQ_A), cols]
            s = lax.dot_general(q, k, (((1,), (1,)), ((), ())), preferred_element_type=F32)
            s = s * (SCALE * LOG2E) + bias_ref[u, :, blk * TQ_A:(blk + 1) * TQ_A]
            if pens[blk] is not None:
                s = s + pens[blk]
            parts.append(s)
        s = jnp.concatenate(parts, axis=1)
        m = jnp.max(s, axis=-1, keepdims=True)
        p = jnp.exp2(s - m)
        l = jnp.sum(p, axis=-1, keepdims=True)
        v = jnp.concatenate([v_ref[pl.ds(st, TQ_A), cols] for st in starts], axis=0)
        o = jnp.dot(p.astype(BF16), v, preferred_element_type=F32)
        o_ref[:, cols] = (o / l).astype(o_ref.dtype)


def _attn_a(qkv, e_tab, batch, seq):
    nq = seq // TQ_A
    hb = HEADS_PER_STEP
    w = hb * HEAD_DIM
    ng = N_HEADS // hb
    return pl.pallas_call(
        _attn_a_kernel,
        grid=(batch, ng, nq),
        in_specs=[pl.BlockSpec((hb, 1, E_LEN), lambda b, g, qi: (g, 0, 0)),
                  pl.BlockSpec((TQ_A, w), lambda b, g, qi: (b * nq + qi, g)),
                  pl.BlockSpec((seq, w), lambda b, g, qi: (b, ng + g)),
                  pl.BlockSpec((seq, w), lambda b, g, qi: (b, 2 * ng + g))],
        out_specs=pl.BlockSpec((TQ_A, w), lambda b, g, qi: (b * nq + qi, g)),
        out_shape=jax.ShapeDtypeStruct((batch * seq, WIDTH), BF16),
        scratch_shapes=[pltpu.VMEM((hb, TQ_A, KB_A), F32)],
        compiler_params=_params("arbitrary", "arbitrary", "arbitrary"),
        name="attn_a",
    )(e_tab, qkv, qkv, qkv)


def _fox_kernel(q_ref, k_ref, v_ref, c_ref, ct_ref, o_ref):
    g = pl.program_id(1)
    qi = pl.program_id(2)
    hb = HEADS_PER_STEP
    lane = lax.broadcasted_iota(jnp.int32, c_ref.shape, 1)
    c_blk = c_ref[...]
    cq = [jnp.sum(jnp.where(lane == g * hb + u, c_blk, 0.0), axis=-1, keepdims=True)
          for u in range(hb)]

    def step(j, carry, masked):
        start = pl.multiple_of(j * T_B, T_B)
        out = []
        for u in range(hb):
            m, l, acc = carry[u]
            cols = slice(u * HEAD_DIM, (u + 1) * HEAD_DIM)
            k = k_ref[pl.ds(start, T_B), cols]
            v = v_ref[pl.ds(start, T_B), cols]
            a = lax.dot_general(q_ref[:, cols], k, (((1,), (1,)), ((), ())),
                                preferred_element_type=F32)
            a = a * (SCALE * LOG2E) - ct_ref[u, :, pl.ds(start, T_B)]
            if masked:
                r = lax.broadcasted_iota(jnp.int32, a.shape, 0)
                c = lax.broadcasted_iota(jnp.int32, a.shape, 1)
                a = jnp.where(c <= r, a, NEG_INF)
            m_new = jnp.maximum(m, jnp.max(a, axis=-1, keepdims=True) + cq[u])
            alpha = jnp.exp2(m - m_new)
            p = jnp.exp2(a + (cq[u] - m_new))
            l = alpha * l + jnp.sum(p, axis=-1, keepdims=True)
            acc = alpha * acc + jnp.dot(p.astype(BF16), v, preferred_element_type=F32)
            out.append((m_new, l, acc))
        return tuple(out)

    init = tuple((jnp.full((T_B, 1), NEG_INF, F32), jnp.zeros((T_B, 1), F32),
                  jnp.zeros((T_B, HEAD_DIM), F32)) for _ in range(hb))
    carry = lax.fori_loop(0, qi, functools.partial(step, masked=False), init)
    carry = step(qi, carry, masked=True)
    for u in range(hb):
        _, l, acc = carry[u]
        o_ref[:, u * HEAD_DIM:(u + 1) * HEAD_DIM] = (acc / l).astype(o_ref.dtype)


def _fox(qkv, c, ct, batch, seq):
    nq = seq // T_B
    hb = HEADS_PER_STEP
    w = hb * HEAD_DIM
    ng = N_HEADS // hb
    return pl.pallas_call(
        _fox_kernel,
        grid=(batch, ng, nq),
        in_specs=[pl.BlockSpec((T_B, w), lambda b, g, qi: (b * nq + qi, 3 * ng + g)),
                  pl.BlockSpec((seq, w), lambda b, g, qi: (b, 4 * ng + g)),
                  pl.BlockSpec((seq, w), lambda b, g, qi: (b, 5 * ng + g)),
                  pl.BlockSpec((T_B, LANES), lambda b, g, qi: (b * nq + qi, 0)),
                  pl.BlockSpec((hb, 1, seq), lambda b, g, qi: (b * ng + g, 0, 0))],
        out_specs=pl.BlockSpec((T_B, w), lambda b, g, qi: (b * nq + qi, g)),
        out_shape=jax.ShapeDtypeStruct((batch * seq, WIDTH), BF16),
        compiler_params=_params("parallel", "parallel", "parallel"),
        name="fox",
    )(qkv, qkv, qkv, c, ct)


def _relpos_offset_vector(rel_bias):
    far = jnp.broadcast_to(rel_bias[:, 2 * MAX_REL:], (rel_bias.shape[0], E_LEN - 2 * MAX_REL))
    near = rel_bias[:, :0:-1]
    return jnp.concatenate([far, near], axis=1).astype(F32)[:, None, :]


def kernel(x, g_mix, w_in, b_f, b_gate, rel_bias, w_branch_a, w_branch_b, w_out,
           g_ffn, w_gate_ffn, w_up_ffn, w_down_ffn, g_final):
    batch, seq, d = x.shape
    m = batch * seq
    depth = g_mix.shape[0]
    qkv_w = 6 * WIDTH
    x2 = x.reshape(m, d)
    for l in range(depth):
        w_f = jnp.pad(w_in[l, :, qkv_w:qkv_w + N_HEADS], ((0, 0), (0, LANES - N_HEADS)))
        w_g = w_in[l, :, qkv_w + N_HEADS:]
        b_f_pad = jnp.pad(b_f[l], (0, LANES - N_HEADS)).reshape(1, LANES)

        h = _rmsnorm(x2, g_mix[l], BF16)
        qkv = _matmul(h, w_in[l], 0, qkv_w, BF16, 1024, 512, "proj_qkv")
        f_pad = _matmul(h, w_f, 0, LANES, F32, 1024, LANES, "proj_forget")
        gates = _matmul(h, w_g, 0, 2 * d, F32, 1024, 512, "proj_gates")

        c, ct = _forget_scan(f_pad, b_f_pad, batch, seq)
        ct = ct[:, :N_HEADS, :].reshape(batch * N_HEADS, 1, seq)

        o_a = _attn_a(qkv, _relpos_offset_vector(rel_bias[l]), batch, seq)
        o_b = _fox(qkv, c, ct, batch, seq)

        merged = _merge(o_a, o_b, w_branch_a[l], w_branch_b[l], gates, b_gate[l])
        x2 = _matmul_residual(merged, w_out[l], x2, 1024, 512, "out_proj")

        h2 = _rmsnorm(x2, g_ffn[l], BF16)
        hidden = _swiglu_up(h2, w_gate_ffn[l], w_up_ffn[l])
        x2 = _matmul_residual_bf16(hidden, w_down_ffn[l].astype(BF16), x2, 512, 512, "ffn_down")
    return _rmsnorm(x2, g_final, F32).reshape(batch, seq, d)
```

```python
import functools
import math

import jax
import jax.numpy as jnp
from jax import lax
from jax.experimental import pallas as pl
from jax.experimental.pallas import tpu as pltpu

F32 = jnp.float32
BF16 = jnp.bfloat16

CHUNK = 64
LEFT_CHUNKS = 8
HEAD_DIM = 128
N_HEADS = 16
WIDTH = N_HEADS * HEAD_DIM
MAX_REL = 256
RMS_EPS = 1e-6
NEG_INF = -1e30
SCALE = HEAD_DIM ** -0.5
LOG2E = math.log2(math.e)

LANES = 128
VMEM_LIMIT = 56 * 1024 * 1024

HEADS_PER_STEP = 2
TQ_A = 256
KB_A = 3 * TQ_A
E_LEN = 4 * TQ_A
T_B = 512


def _params(*sem):
    return pltpu.CompilerParams(dimension_semantics=sem, vmem_limit_bytes=VMEM_LIMIT)


def _rmsnorm_kernel(x_ref, g_ref, o_ref):
    x = x_ref[...]
    y = x * lax.rsqrt(jnp.mean(x * x, axis=-1, keepdims=True) + RMS_EPS)
    o_ref[...] = (y * g_ref[...]).astype(o_ref.dtype)


def _rmsnorm(x, g, out_dtype, tm=256):
    m, d = x.shape
    return pl.pallas_call(
        _rmsnorm_kernel,
        grid=(m // tm,),
        in_specs=[pl.BlockSpec((tm, d), lambda i: (i, 0)),
                  pl.BlockSpec((1, d), lambda i: (0, 0))],
        out_specs=pl.BlockSpec((tm, d), lambda i: (i, 0)),
        out_shape=jax.ShapeDtypeStruct((m, d), out_dtype),
        compiler_params=_params("parallel"),
        name="rmsnorm",
    )(x, g.reshape(1, d))


def _cast_weights(pairs):
    @pl.when(pl.program_id(1) == 0)
    def _():
        for w_ref, wq_ref in pairs:
            wq_ref[...] = w_ref[...].astype(BF16)


def _mm_kernel(a_ref, b_ref, o_ref, bq_ref):
    _cast_weights([(b_ref, bq_ref)])
    o_ref[...] = jnp.dot(a_ref[...], bq_ref[...],
                         preferred_element_type=F32).astype(o_ref.dtype)


def _matmul(a, w, col0, n, out_dtype, tm, tn, name):
    m, k = a.shape
    cb = col0 // tn
    return pl.pallas_call(
        _mm_kernel,
        grid=(n // tn, m // tm),
        in_specs=[pl.BlockSpec((tm, k), lambda j, i: (i, 0)),
                  pl.BlockSpec((k, tn), lambda j, i: (0, j + cb))],
        out_specs=pl.BlockSpec((tm, tn), lambda j, i: (i, j)),
        out_shape=jax.ShapeDtypeStruct((m, n), out_dtype),
        scratch_shapes=[pltpu.VMEM((k, tn), BF16)],
        compiler_params=_params("arbitrary", "arbitrary"),
        name=name,
    )(a, w)


def _mm_nt_kernel(a_ref, bt_ref, o_ref, bq_ref):
    _cast_weights([(bt_ref, bq_ref)])
    o_ref[...] = lax.dot_general(a_ref[...], bq_ref[...], (((1,), (1,)), ((), ())),
                                 preferred_element_type=F32).astype(o_ref.dtype)


def _matmul_nt(a, wt, row0, n, out_dtype, tm, tn, name):
    m, k = a.shape
    if row0 % tn == 0:
        rb = row0 // tn
        w_spec = pl.BlockSpec((tn, k), lambda j, i: (j + rb, 0))
    else:
        w_spec = pl.BlockSpec((pl.Element(tn), pl.Element(k)),
                              lambda j, i: (pl.multiple_of(row0 + j * tn, math.gcd(row0, tn)), 0))
    return pl.pallas_call(
        _mm_nt_kernel,
        grid=(n // tn, m // tm),
        in_specs=[pl.BlockSpec((tm, k), lambda j, i: (i, 0)), w_spec],
        out_specs=pl.BlockSpec((tm, tn), lambda j, i: (i, j)),
        out_shape=jax.ShapeDtypeStruct((m, n), out_dtype),
        scratch_shapes=[pltpu.VMEM((tn, k), BF16)],
        compiler_params=_params("arbitrary", "arbitrary"),
        name=name,
    )(a, wt)


def _mm_res_kernel(a_ref, b_ref, r_ref, o_ref, bq_ref):
    _cast_weights([(b_ref, bq_ref)])
    o_ref[...] = r_ref[...] + jnp.dot(a_ref[...], bq_ref[...], preferred_element_type=F32)


def _matmul_residual(a, w, r, tm, tn, name):
    m, k = a.shape
    n = w.shape[1]
    return pl.pallas_call(
        _mm_res_kernel,
        grid=(n // tn, m // tm),
        in_specs=[pl.BlockSpec((tm, k), lambda j, i: (i, 0)),
                  pl.BlockSpec((k, tn), lambda j, i: (0, j)),
                  pl.BlockSpec((tm, tn), lambda j, i: (i, j))],
        out_specs=pl.BlockSpec((tm, tn), lambda j, i: (i, j)),
        out_shape=jax.ShapeDtypeStruct((m, n), F32),
        scratch_shapes=[pltpu.VMEM((k, tn), BF16)],
        compiler_params=_params("arbitrary", "arbitrary"),
        name=name,
    )(a, w, r)


def _mm_res_bf16_kernel(a_ref, b_ref, r_ref, o_ref):
    o_ref[...] = r_ref[...] + jnp.dot(a_ref[...], b_ref[...], preferred_element_type=F32)


def _matmul_residual_bf16(a, b, r, tm, tn, name):
    m, k = a.shape
    n = b.shape[1]
    return pl.pallas_call(
        _mm_res_bf16_kernel,
        grid=(m // tm, n // tn),
        in_specs=[pl.BlockSpec((tm, k), lambda i, j: (i, 0)),
                  pl.BlockSpec((k, tn), lambda i, j: (0, j)),
                  pl.BlockSpec((tm, tn), lambda i, j: (i, j))],
        out_specs=pl.BlockSpec((tm, tn), lambda i, j: (i, j)),
        out_shape=jax.ShapeDtypeStruct((m, n), F32),
        compiler_params=_params("parallel", "parallel"),
        name=name,
    )(a, b, r)


def _merge_kernel(oa_ref, ob_ref, wa_ref, wb_ref, ga_ref, gb_ref, ba_ref, bb_ref, o_ref,
                  waq_ref, wbq_ref):
    _cast_weights([(wa_ref, waq_ref), (wb_ref, wbq_ref)])
    ua = jnp.dot(oa_ref[...], waq_ref[...], preferred_element_type=F32)
    ub = jnp.dot(ob_ref[...], wbq_ref[...], preferred_element_type=F32)
    merged = (jax.nn.sigmoid(ga_ref[...] + ba_ref[...]) * ua
              + jax.nn.sigmoid(gb_ref[...] + bb_ref[...]) * ub)
    o_ref[...] = merged.astype(o_ref.dtype)


def _merge(o_a, o_b, w_a, w_b, gates, b_gate, tm=1024, tn=512):
    m, k = o_a.shape
    d = w_a.shape[1]
    nb = d // tn
    bg = b_gate.reshape(1, 2 * d)
    return pl.pallas_call(
        _merge_kernel,
        grid=(nb, m // tm),
        in_specs=[pl.BlockSpec((tm, k), lambda j, i: (i, 0)),
                  pl.BlockSpec((tm, k), lambda j, i: (i, 0)),
                  pl.BlockSpec((k, tn), lambda j, i: (0, j)),
                  pl.BlockSpec((k, tn), lambda j, i: (0, j)),
                  pl.BlockSpec((tm, tn), lambda j, i: (i, j)),
                  pl.BlockSpec((tm, tn), lambda j, i: (i, j + nb)),
                  pl.BlockSpec((1, tn), lambda j, i: (0, j)),
                  pl.BlockSpec((1, tn), lambda j, i: (0, j + nb))],
        out_specs=pl.BlockSpec((tm, tn), lambda j, i: (i, j)),
        out_shape=jax.ShapeDtypeStruct((m, d), BF16),
        scratch_shapes=[pltpu.VMEM((k, tn), BF16), pltpu.VMEM((k, tn), BF16)],
        compiler_params=_params("arbitrary", "arbitrary"),
        name="merge",
    )(o_a, o_b, w_a, w_b, gates, gates, bg, bg)


def _swiglu_up_kernel(a_ref, wg_ref, wu_ref, o_ref, wgq_ref, wuq_ref):
    _cast_weights([(wg_ref, wgq_ref), (wu_ref, wuq_ref)])
    a = a_ref[...]
    g = jnp.dot(a, wgq_ref[...], preferred_element_type=F32)
    u = jnp.dot(a, wuq_ref[...], preferred_element_type=F32)
    o_ref[...] = (g * jax.nn.sigmoid(g) * u).astype(o_ref.dtype)


def _swiglu_up(a, wg, wu, tm=1024, tn=256):
    m, k = a.shape
    f = wg.shape[1]
    return pl.pallas_call(
        _swiglu_up_kernel,
        grid=(f // tn, m // tm),
        in_specs=[pl.BlockSpec((tm, k), lambda j, i: (i, 0)),
                  pl.BlockSpec((k, tn), lambda j, i: (0, j)),
                  pl.BlockSpec((k, tn), lambda j, i: (0, j))],
        out_specs=pl.BlockSpec((tm, tn), lambda j, i: (i, j)),
        out_shape=jax.ShapeDtypeStruct((m, f), BF16),
        scratch_shapes=[pltpu.VMEM((k, tn), BF16), pltpu.VMEM((k, tn), BF16)],
        compiler_params=_params("arbitrary", "arbitrary"),
        name="swiglu_up",
    )(a, wg, wu)


def _cum_kernel(f_ref, b_ref, c_ref, ct_ref):
    x = f_ref[...] + b_ref[...]
    c = jnp.minimum(x, 0.0) - jnp.log1p(jnp.exp(-jnp.abs(x)))
    n = c.shape[0]
    row = lax.broadcasted_iota(jnp.int32, c.shape, 0)
    shift = 1
    while shift < n:
        c = c + jnp.where(row >= shift, pltpu.roll(c, shift, axis=0), 0.0)
        shift *= 2
    c = c * LOG2E
    c_ref[...] = c
    ct_ref[...] = c.T


def _forget_scan(f_pad, b_pad, batch, seq):
    return pl.pallas_call(
        _cum_kernel,
        grid=(batch,),
        in_specs=[pl.BlockSpec((seq, LANES), lambda b: (b, 0)),
                  pl.BlockSpec((1, LANES), lambda b: (0, 0))],
        out_specs=[pl.BlockSpec((seq, LANES), lambda b: (b, 0)),
                   pl.BlockSpec((None, LANES, seq), lambda b: (b, 0, 0))],
        out_shape=[jax.ShapeDtypeStruct((batch * seq, LANES), F32),
                   jax.ShapeDtypeStruct((batch, LANES, seq), F32)],
        compiler_params=_params("parallel"),
        name="forget_scan",
    )(f_pad, b_pad)


def _attn_a_kernel(e_ref, q_ref, k_ref, v_ref, o_ref, bias_ref):
    qi = pl.program_id(2)

    @pl.when(qi == 0)
    def _():
        qc = lax.broadcasted_iota(jnp.int32, (TQ_A, KB_A), 0) // CHUNK
        kc = lax.broadcasted_iota(jnp.int32, (TQ_A, KB_A), 1) // CHUNK
        valid = (kc >= qc) & (kc <= qc + LEFT_CHUNKS)
        for u in range(HEADS_PER_STEP):
            e = jnp.broadcast_to(e_ref[u], (TQ_A, E_LEN))
            t = pltpu.roll(e, 0, 1, stride=1, stride_axis=0)[:, E_LEN - KB_A:]
            bias_ref[u] = jnp.where(valid, t, NEG_INF) * LOG2E

    starts = [pl.multiple_of(jnp.maximum(qi - back, 0) * TQ_A, TQ_A) for back in (2, 1, 0)]
    pens = [jnp.where(qi >= 2, 0.0, NEG_INF), jnp.where(qi >= 1, 0.0, NEG_INF), None]
    for u in range(HEADS_PER_STEP):
        cols = slice(u * HEAD_DIM, (u + 1) * HEAD_DIM)
        q = q_ref[:, cols]
        parts = []
        for blk in range(3):
            k = k_ref[pl.ds(starts[blk], TQ_A), cols]
            s = lax.dot_general(q, k, (((1,), (1,)), ((), ())), preferred_element_type=F32)
            s = s * (SCALE * LOG2E) + bias_ref[u, :, blk * TQ_A:(blk + 1) * TQ_A]
            if pens[blk] is not None:
                s = s + pens[blk]
            parts.append(s)
        s = jnp.concatenate(parts, axis=1)
        m = jnp.max(s, axis=-1, keepdims=True)
        p = jnp.exp2(s - m)
        l = jnp.sum(p, axis=-1, keepdims=True)
        v = jnp.concatenate([v_ref[pl.ds(st, TQ_A), cols] for st in starts], axis=0)
        o = jnp.dot(p.astype(BF16), v, preferred_element_type=F32)
        o_ref[:, cols] = (o / l).astype(o_ref.dtype)


def _attn_a(qkv, e_tab, batch, seq):
    nq = seq // TQ_A
    hb = HEADS_PER_STEP
    w = hb * HEAD_DIM
    ng = N_HEADS // hb
    return pl.pallas_call(
        _attn_a_kernel,
        grid=(batch, ng, nq),
        in_specs=[pl.BlockSpec((hb, 1, E_LEN), lambda b, g, qi: (g, 0, 0)),
                  pl.BlockSpec((TQ_A, w), lambda b, g, qi: (b * nq + qi, g)),
                  pl.BlockSpec((seq, w), lambda b, g, qi: (b, ng + g)),
                  pl.BlockSpec((seq, w), lambda b, g, qi: (b, 2 * ng + g))],
        out_specs=pl.BlockSpec((TQ_A, w), lambda b, g, qi: (b * nq + qi, g)),
        out_shape=jax.ShapeDtypeStruct((batch * seq, WIDTH), BF16),
        scratch_shapes=[pltpu.VMEM((hb, TQ_A, KB_A), F32)],
        compiler_params=_params("arbitrary", "arbitrary", "arbitrary"),
        name="attn_a",
    )(e_tab, qkv, qkv, qkv)


def _fox_kernel(q_ref, k_ref, v_ref, c_ref, ct_ref, o_ref):
    g = pl.program_id(1)
    qi = pl.program_id(2)
    hb = HEADS_PER_STEP
    lane = lax.broadcasted_iota(jnp.int32, c_ref.shape, 1)
    c_blk = c_ref[...]
    cq = [jnp.sum(jnp.where(lane == g * hb + u, c_blk, 0.0), axis=-1, keepdims=True)
          for u in range(hb)]

    def step(j, carry, masked):
        start = pl.multiple_of(j * T_B, T_B)
        out = []
        for u in range(hb):
            m, l, acc = carry[u]
            cols = slice(u * HEAD_DIM, (u + 1) * HEAD_DIM)
            k = k_ref[pl.ds(start, T_B), cols]
            v = v_ref[pl.ds(start, T_B), cols]
            a = lax.dot_general(q_ref[:, cols], k, (((1,), (1,)), ((), ())),
                                preferred_element_type=F32)
            a = a * (SCALE * LOG2E) - ct_ref[u, :, pl.ds(start, T_B)]
            if masked:
                r = lax.broadcasted_iota(jnp.int32, a.shape, 0)
                c = lax.broadcasted_iota(jnp.int32, a.shape, 1)
                a = jnp.where(c <= r, a, NEG_INF)
            m_new = jnp.maximum(m, jnp.max(a, axis=-1, keepdims=True) + cq[u])
            alpha = jnp.exp2(m - m_new)
            p = jnp.exp2(a + (cq[u] - m_new))
            l = alpha * l + jnp.sum(p, axis=-1, keepdims=True)
            acc = alpha * acc + jnp.dot(p.astype(BF16), v, preferred_element_type=F32)
            out.append((m_new, l, acc))
        return tuple(out)

    init = tuple((jnp.full((T_B, 1), NEG_INF, F32), jnp.zeros((T_B, 1), F32),
                  jnp.zeros((T_B, HEAD_DIM), F32)) for _ in range(hb))
    carry = lax.fori_loop(0, qi, functools.partial(step, masked=False), init)
    carry = step(qi, carry, masked=True)
    for u in range(hb):
        _, l, acc = carry[u]
        o_ref[:, u * HEAD_DIM:(u + 1) * HEAD_DIM] = (acc / l).astype(o_ref.dtype)


def _fox(qkv, c, ct, batch, seq):
    nq = seq // T_B
    hb = HEADS_PER_STEP
    w = hb * HEAD_DIM
    ng = N_HEADS // hb
    return pl.pallas_call(
        _fox_kernel,
        grid=(batch, ng, nq),
        in_specs=[pl.BlockSpec((T_B, w), lambda b, g, qi: (b * nq + qi, 3 * ng + g)),
                  pl.BlockSpec((seq, w), lambda b, g, qi: (b, 4 * ng + g)),
                  pl.BlockSpec((seq, w), lambda b, g, qi: (b, 5 * ng + g)),
                  pl.BlockSpec((T_B, LANES), lambda b, g, qi: (b * nq + qi, 0)),
                  pl.BlockSpec((hb, 1, seq), lambda b, g, qi: (b * ng + g, 0, 0))],
        out_specs=pl.BlockSpec((T_B, w), lambda b, g, qi: (b * nq + qi, g)),
        out_shape=jax.ShapeDtypeStruct((batch * seq, WIDTH), BF16),
        compiler_params=_params("parallel", "parallel", "parallel"),
        name="fox",
    )(qkv, qkv, qkv, c, ct)


def _relpos_offset_vector(rel_bias):
    far = jnp.broadcast_to(rel_bias[:, 2 * MAX_REL:], (rel_bias.shape[0], E_LEN - 2 * MAX_REL))
    near = rel_bias[:, :0:-1]
    return jnp.concatenate([far, near], axis=1).astype(F32)[:, None, :]


def kernel(x, g_mix, w_in, b_f, b_gate, rel_bias, w_branch_a, w_branch_b, w_out,
           g_ffn, w_gate_ffn, w_up_ffn, w_down_ffn, g_final):
    batch, seq, d = x.shape
    m = batch * seq
    depth = g_mix.shape[0]
    qkv_w = 6 * WIDTH
    x2 = x.reshape(m, d)
    for l in range(depth):
        w_in_t = jnp.swapaxes(w_in[l], 0, 1)
        b_f_pad = jnp.pad(b_f[l], (0, LANES - N_HEADS)).reshape(1, LANES)

        h = _rmsnorm(x2, g_mix[l], BF16)
        qkv = _matmul_nt(h, w_in_t, 0, qkv_w, BF16, 1024, 512, "proj_qkv")
        f_pad = _matmul_nt(h, w_in_t, qkv_w, LANES, F32, 1024, LANES, "proj_forget")
        gates = _matmul_nt(h, w_in_t, qkv_w + N_HEADS, 2 * d, F32, 1024, 512, "proj_gates")

        c, ct = _forget_scan(f_pad, b_f_pad, batch, seq)
        ct = ct[:, :N_HEADS, :].reshape(batch * N_HEADS, 1, seq)

        o_a = _attn_a(qkv, _relpos_offset_vector(rel_bias[l]), batch, seq)
        o_b = _fox(qkv, c, ct, batch, seq)

        merged = _merge(o_a, o_b, w_branch_a[l], w_branch_b[l], gates, b_gate[l])
        x2 = _matmul_residual(merged, w_out[l], x2, 1024, 512, "out_proj")

        h2 = _rmsnorm(x2, g_ffn[l], BF16)
        hidden = _swiglu_up(h2, w_gate_ffn[l], w_up_ffn[l])
        x2 = _matmul_residual_bf16(hidden, w_down_ffn[l].astype(BF16), x2, 512, 512, "ffn_down")
    return _rmsnorm(x2, g_final, F32).reshape(batch, seq, d)
```

```python
import functools
import math

import jax
import jax.numpy as jnp
from jax import lax
from jax.experimental import pallas as pl
from jax.experimental.pallas import tpu as pltpu

F32 = jnp.float32
BF16 = jnp.bfloat16

CHUNK = 64
LEFT_CHUNKS = 8
HEAD_DIM = 128
N_HEADS = 16
WIDTH = N_HEADS * HEAD_DIM
MAX_REL = 256
RMS_EPS = 1e-6
NEG_INF = -1e30
SCALE = HEAD_DIM ** -0.5
LOG2E = math.log2(math.e)

LANES = 128
VMEM_LIMIT = 56 * 1024 * 1024

HEADS_PER_STEP = 2
TQ_A = 256
KB_A = 3 * TQ_A
E_LEN = 4 * TQ_A
T_B = 512
PROJ_TN = 512


def _params(*sem):
    return pltpu.CompilerParams(dimension_semantics=sem, vmem_limit_bytes=VMEM_LIMIT)


def _rmsnorm_kernel(x_ref, g_ref, o_ref):
    x = x_ref[...]
    y = x * lax.rsqrt(jnp.mean(x * x, axis=-1, keepdims=True) + RMS_EPS)
    o_ref[...] = (y * g_ref[...]).astype(o_ref.dtype)


def _rmsnorm(x, g, out_dtype, tm=256):
    m, d = x.shape
    return pl.pallas_call(
        _rmsnorm_kernel,
        grid=(m // tm,),
        in_specs=[pl.BlockSpec((tm, d), lambda i: (i, 0)),
                  pl.BlockSpec((1, d), lambda i: (0, 0))],
        out_specs=pl.BlockSpec((tm, d), lambda i: (i, 0)),
        out_shape=jax.ShapeDtypeStruct((m, d), out_dtype),
        compiler_params=_params("parallel"),
        name="rmsnorm",
    )(x, g.reshape(1, d))


def _cast_weights(pairs):
    @pl.when(pl.program_id(1) == 0)
    def _():
        for w_ref, wq_ref in pairs:
            wq_ref[...] = w_ref[...].astype(BF16)


_NT = (((1,), (1,)), ((), ()))


def _mm_wt_kernel(a_ref, wt_ref, o_ref, wq_ref, *, transpose_out):
    _cast_weights([(wt_ref, wq_ref)])
    if transpose_out:
        acc = lax.dot_general(wq_ref[...], a_ref[...], _NT, preferred_element_type=F32)
    else:
        acc = lax.dot_general(a_ref[...], wq_ref[...], _NT, preferred_element_type=F32)
    o_ref[...] = acc.astype(o_ref.dtype)


def _matmul_wt(a, wt, row_of, n, out_dtype, tm, tn, name, transpose_out=False):
    m, k = a.shape
    w_spec = pl.BlockSpec((pl.Element(tn), pl.Element(k)),
                          lambda j, i: (pl.multiple_of(row_of(j), 8), 0))
    if transpose_out:
        out_spec = pl.BlockSpec((tn, tm), lambda j, i: (j, i))
        out_shape = jax.ShapeDtypeStruct((n, m), out_dtype)
    else:
        out_spec = pl.BlockSpec((tm, tn), lambda j, i: (i, j))
        out_shape = jax.ShapeDtypeStruct((m, n), out_dtype)
    return pl.pallas_call(
        functools.partial(_mm_wt_kernel, transpose_out=transpose_out),
        grid=(n // tn, m // tm),
        in_specs=[pl.BlockSpec((tm, k), lambda j, i: (i, 0)), w_spec],
        out_specs=out_spec,
        out_shape=out_shape,
        scratch_shapes=[pltpu.VMEM((tn, k), BF16)],
        compiler_params=_params("arbitrary", "arbitrary"),
        name=name,
    )(a, wt)


def _mm_res_kernel(a_ref, b_ref, r_ref, o_ref, bq_ref):
    _cast_weights([(b_ref, bq_ref)])
    o_ref[...] = r_ref[...] + jnp.dot(a_ref[...], bq_ref[...], preferred_element_type=F32)


def _matmul_residual(a, w, r, tm, tn, name):
    m, k = a.shape
    n = w.shape[1]
    return pl.pallas_call(
        _mm_res_kernel,
        grid=(n // tn, m // tm),
        in_specs=[pl.BlockSpec((tm, k), lambda j, i: (i, 0)),
                  pl.BlockSpec((k, tn), lambda j, i: (0, j)),
                  pl.BlockSpec((tm, tn), lambda j, i: (i, j))],
        out_specs=pl.BlockSpec((tm, tn), lambda j, i: (i, j)),
        out_shape=jax.ShapeDtypeStruct((m, n), F32),
        scratch_shapes=[pltpu.VMEM((k, tn), BF16)],
        compiler_params=_params("arbitrary", "arbitrary"),
        name=name,
    )(a, w, r)


def _mm_res_bf16_kernel(a_ref, b_ref, r_ref, o_ref):
    o_ref[...] = r_ref[...] + jnp.dot(a_ref[...], b_ref[...], preferred_element_type=F32)


def _matmul_residual_bf16(a, b, r, tm, tn, name):
    m, k = a.shape
    n = b.shape[1]
    return pl.pallas_call(
        _mm_res_bf16_kernel,
        grid=(m // tm, n // tn),
        in_specs=[pl.BlockSpec((tm, k), lambda i, j: (i, 0)),
                  pl.BlockSpec((k, tn), lambda i, j: (0, j)),
                  pl.BlockSpec((tm, tn), lambda i, j: (i, j))],
        out_specs=pl.BlockSpec((tm, tn), lambda i, j: (i, j)),
        out_shape=jax.ShapeDtypeStruct((m, n), F32),
        compiler_params=_params("parallel", "parallel"),
        name=name,
    )(a, b, r)


def _merge_kernel(oa_ref, ob_ref, wa_ref, wb_ref, ga_ref, gb_ref, ba_ref, bb_ref, o_ref,
                  waq_ref, wbq_ref):
    _cast_weights([(wa_ref, waq_ref), (wb_ref, wbq_ref)])
    ua = jnp.dot(oa_ref[...], waq_ref[...], preferred_element_type=F32)
    ub = jnp.dot(ob_ref[...], wbq_ref[...], preferred_element_type=F32)
    merged = (jax.nn.sigmoid(ga_ref[...] + ba_ref[...]) * ua
              + jax.nn.sigmoid(gb_ref[...] + bb_ref[...]) * ub)
    o_ref[...] = merged.astype(o_ref.dtype)


def _merge(o_a, o_b, w_a, w_b, gates, b_gate, tm=1024, tn=512):
    m, k = o_a.shape
    d = w_a.shape[1]
    nb = d // tn
    bg = b_gate.reshape(1, 2 * d)
    return pl.pallas_call(
        _merge_kernel,
        grid=(nb, m // tm),
        in_specs=[pl.BlockSpec((tm, k), lambda j, i: (i, 0)),
                  pl.BlockSpec((tm, k), lambda j, i: (i, 0)),
                  pl.BlockSpec((k, tn), lambda j, i: (0, j)),
                  pl.BlockSpec((k, tn), lambda j, i: (0, j)),
                  pl.BlockSpec((tm, tn), lambda j, i: (i, j)),
                  pl.BlockSpec((tm, tn), lambda j, i: (i, j + nb)),
                  pl.BlockSpec((1, tn), lambda j, i: (0, j)),
                  pl.BlockSpec((1, tn), lambda j, i: (0, j + nb))],
        out_specs=pl.BlockSpec((tm, tn), lambda j, i: (i, j)),
        out_shape=jax.ShapeDtypeStruct((m, d), BF16),
        scratch_shapes=[pltpu.VMEM((k, tn), BF16), pltpu.VMEM((k, tn), BF16)],
        compiler_params=_params("arbitrary", "arbitrary"),
        name="merge",
    )(o_a, o_b, w_a, w_b, gates, gates, bg, bg)


def _swiglu_up_kernel(a_ref, wg_ref, wu_ref, o_ref, wgq_ref, wuq_ref):
    _cast_weights([(wg_ref, wgq_ref), (wu_ref, wuq_ref)])
    a = a_ref[...]
    g = jnp.dot(a, wgq_ref[...], preferred_element_type=F32)
    u = jnp.dot(a, wuq_ref[...], preferred_element_type=F32)
    o_ref[...] = (g * jax.nn.sigmoid(g) * u).astype(o_ref.dtype)


def _swiglu_up(a, wg, wu, tm=1024, tn=256):
    m, k = a.shape
    f = wg.shape[1]
    return pl.pallas_call(
        _swiglu_up_kernel,
        grid=(f // tn, m // tm),
        in_specs=[pl.BlockSpec((tm, k), lambda j, i: (i, 0)),
                  pl.BlockSpec((k, tn), lambda j, i: (0, j)),
                  pl.BlockSpec((k, tn), lambda j, i: (0, j))],
        out_specs=pl.BlockSpec((tm, tn), lambda j, i: (i, j)),
        out_shape=jax.ShapeDtypeStruct((m, f), BF16),
        scratch_shapes=[pltpu.VMEM((k, tn), BF16), pltpu.VMEM((k, tn), BF16)],
        compiler_params=_params("arbitrary", "arbitrary"),
        name="swiglu_up",
    )(a, wg, wu)


def _cum_kernel(f_ref, b_ref, c_ref, ct_ref):
    x = f_ref[...] + b_ref[...]
    c = jnp.minimum(x, 0.0) - jnp.log1p(jnp.exp(-jnp.abs(x)))
    n = c.shape[0]
    row = lax.broadcasted_iota(jnp.int32, c.shape, 0)
    shift = 1
    while shift < n:
        c = c + jnp.where(row >= shift, pltpu.roll(c, shift, axis=0), 0.0)
        shift *= 2
    c = c * LOG2E
    c_ref[...] = c
    ct_ref[...] = c.T


def _forget_scan(f_pad, b_pad, batch, seq):
    return pl.pallas_call(
        _cum_kernel,
        grid=(batch,),
        in_specs=[pl.BlockSpec((seq, LANES), lambda b: (b, 0)),
                  pl.BlockSpec((1, LANES), lambda b: (0, 0))],
        out_specs=[pl.BlockSpec((seq, LANES), lambda b: (b, 0)),
                   pl.BlockSpec((None, LANES, seq), lambda b: (b, 0, 0))],
        out_shape=[jax.ShapeDtypeStruct((batch * seq, LANES), F32),
                   jax.ShapeDtypeStruct((batch, LANES, seq), F32)],
        compiler_params=_params("parallel"),
        name="forget_scan",
    )(f_pad, b_pad)


def _lane_bcast_kernel(c_ref, o_ref):
    c = c_ref[...]
    for hh in range(N_HEADS):
        o_ref[hh] = jnp.broadcast_to(c[:, hh:hh + 1], c.shape)


def _forget_lane_bcast(c, batch, seq, tb=512):
    nb = seq // tb
    return pl.pallas_call(
        _lane_bcast_kernel,
        grid=(batch, nb),
        in_specs=[pl.BlockSpec((tb, LANES), lambda b, t: (b * nb + t, 0))],
        out_specs=pl.BlockSpec((N_HEADS, tb, LANES), lambda b, t: (b, t, 0)),
        out_shape=jax.ShapeDtypeStruct((batch * N_HEADS, seq, LANES), F32),
        compiler_params=_params("parallel", "parallel"),
        name="forget_lane_bcast",
    )(c)


def _head_rows(u):
    return slice(u * HEAD_DIM, (u + 1) * HEAD_DIM)


def _attn_a_kernel(e_ref, qt_ref, k_ref, vt_ref, o_ref, bias_ref):
    qi = pl.program_id(2)

    @pl.when(qi == 0)
    def _():
        qc = lax.broadcasted_iota(jnp.int32, (TQ_A, KB_A), 0) // CHUNK
        kc = lax.broadcasted_iota(jnp.int32, (TQ_A, KB_A), 1) // CHUNK
        valid = (kc >= qc) & (kc <= qc + LEFT_CHUNKS)
        for u in range(HEADS_PER_STEP):
            e = jnp.broadcast_to(e_ref[u], (TQ_A, E_LEN))
            t = pltpu.roll(e, 0, 1, stride=1, stride_axis=0)[:, E_LEN - KB_A:]
            bias_ref[u] = (jnp.where(valid, t, NEG_INF) * LOG2E).T

    starts = [pl.multiple_of(jnp.maximum(qi - back, 0) * TQ_A, TQ_A) for back in (2, 1, 0)]
    pens = [jnp.where(qi >= 2, 0.0, NEG_INF), jnp.where(qi >= 1, 0.0, NEG_INF), None]
    heads = [_head_rows(u) for u in range(HEADS_PER_STEP)]
    scores = []
    for u, hd in enumerate(heads):
        qt = qt_ref[hd, :]
        parts = []
        for blk in range(3):
            k = k_ref[pl.ds(starts[blk], TQ_A), hd]
            s = jnp.dot(k, qt, preferred_element_type=F32)
            s = s * (SCALE * LOG2E) + bias_ref[u, blk * TQ_A:(blk + 1) * TQ_A, :]
            if pens[blk] is not None:
                s = s + pens[blk]
            parts.append(s)
        scores.append(jnp.concatenate(parts, axis=0))
    probs = []
    for s in scores:
        m = jnp.max(s, axis=0, keepdims=True)
        p = jnp.exp2(s - m)
        probs.append((p.astype(BF16), jnp.sum(p, axis=0, keepdims=True)))
    for hd, (p, l) in zip(heads, probs):
        vt = jnp.concatenate([vt_ref[hd, pl.ds(st, TQ_A)] for st in starts], axis=1)
        ot = jnp.dot(vt, p, preferred_element_type=F32)
        o_ref[:, hd] = (ot / l).T.astype(o_ref.dtype)


def _attn_a(qvt, kk, e_tab, batch, seq):
    nq = seq // TQ_A
    hb = HEADS_PER_STEP
    w = hb * HEAD_DIM
    ng = N_HEADS // hb
    return pl.pallas_call(
        _attn_a_kernel,
        grid=(batch, ng, nq),
        in_specs=[pl.BlockSpec((hb, 1, E_LEN), lambda b, g, qi: (g, 0, 0)),
                  pl.BlockSpec((w, TQ_A), lambda b, g, qi: (g, b * nq + qi)),
                  pl.BlockSpec((seq, w), lambda b, g, qi: (b, g)),
                  pl.BlockSpec((w, seq), lambda b, g, qi: (ng + g, b))],
        out_specs=pl.BlockSpec((TQ_A, w), lambda b, g, qi: (b * nq + qi, g)),
        out_shape=jax.ShapeDtypeStruct((batch * seq, WIDTH), BF16),
        scratch_shapes=[pltpu.VMEM((hb, KB_A, TQ_A), F32)],
        compiler_params=_params("arbitrary", "arbitrary", "arbitrary"),
        name="attn_a",
    )(e_tab, qvt, kk, qvt)


def _fox_kernel(qt_ref, k_ref, vt_ref, cb_ref, ct_ref, o_ref):
    qi = pl.program_id(2)
    hb = HEADS_PER_STEP
    q0 = pl.multiple_of(qi * T_B, T_B)

    def step(j, carry, masked):
        start = pl.multiple_of(j * T_B, T_B)
        heads = [_head_rows(u) for u in range(hb)]
        logits = []
        for u, hd in enumerate(heads):
            a = jnp.dot(k_ref[pl.ds(start, T_B), hd], qt_ref[hd, :],
                        preferred_element_type=F32)
            cb = cb_ref[u, pl.ds(start, T_B), :]
            a = a * (SCALE * LOG2E) - jnp.concatenate([cb] * (T_B // LANES), axis=1)
            if masked:
                r = lax.broadcasted_iota(jnp.int32, a.shape, 0)
                c = lax.broadcasted_iota(jnp.int32, a.shape, 1)
                a = jnp.where(r <= c, a, NEG_INF)
            logits.append(a)
        stats = []
        for u, a in enumerate(logits):
            m, l, _ = carry[u]
            cq = ct_ref[u, :, pl.ds(q0, T_B)]
            m_new = jnp.maximum(m, jnp.max(a, axis=0, keepdims=True) + cq)
            alpha = jnp.exp2(m - m_new)
            p = jnp.exp2(a + (cq - m_new))
            l = alpha * l + jnp.sum(p, axis=0, keepdims=True)
            stats.append((m_new, l, alpha, p.astype(BF16)))
        out = []
        for u, hd in enumerate(heads):
            m_new, l, alpha, p = stats[u]
            pv = jnp.dot(vt_ref[hd, pl.ds(start, T_B)], p,
                         preferred_element_type=F32)
            out.append((m_new, l, alpha * carry[u][2] + pv))
        return tuple(out)

    init = tuple((jnp.full((1, T_B), NEG_INF, F32), jnp.zeros((1, T_B), F32),
                  jnp.zeros((HEAD_DIM, T_B), F32)) for _ in range(hb))
    carry = lax.fori_loop(0, qi, functools.partial(step, masked=False), init)
    carry = step(qi, carry, masked=True)
    for u in range(hb):
        _, l, acc = carry[u]
        o_ref[:, _head_rows(u)] = (acc / l).T.astype(o_ref.dtype)


def _fox(qvt, kk, cb, ct, batch, seq):
    nq = seq // T_B
    hb = HEADS_PER_STEP
    w = hb * HEAD_DIM
    ng = N_HEADS // hb
    return pl.pallas_call(
        _fox_kernel,
        grid=(batch, ng, nq),
        in_specs=[pl.BlockSpec((w, T_B), lambda b, g, qi: (2 * ng + g, b * nq + qi)),
                  pl.BlockSpec((seq, w), lambda b, g, qi: (b, ng + g)),
                  pl.BlockSpec((w, seq), lambda b, g, qi: (3 * ng + g, b)),
                  pl.BlockSpec((hb, seq, LANES), lambda b, g, qi: (b * ng + g, 0, 0)),
                  pl.BlockSpec((hb, 1, seq), lambda b, g, qi: (b * ng + g, 0, 0))],
        out_specs=pl.BlockSpec((T_B, w), lambda b, g, qi: (b * nq + qi, g)),
        out_shape=jax.ShapeDtypeStruct((batch * seq, WIDTH), BF16),
        compiler_params=_params("parallel", "parallel", "parallel"),
        name="fox",
    )(qvt, kk, qvt, cb, ct)


def _relpos_offset_vector(rel_bias):
    far = jnp.broadcast_to(rel_bias[:, 2 * MAX_REL:], (rel_bias.shape[0], E_LEN - 2 * MAX_REL))
    near = rel_bias[:, :0:-1]
    return jnp.concatenate([far, near], axis=1).astype(F32)[:, None, :]


def kernel(x, g_mix, w_in, b_f, b_gate, rel_bias, w_branch_a, w_branch_b, w_out,
           g_ffn, w_gate_ffn, w_up_ffn, w_down_ffn, g_final):
    batch, seq, d = x.shape
    m = batch * seq
    depth = g_mix.shape[0]
    qkv_w = 6 * WIDTH
    x2 = x.reshape(m, d)
    for l in range(depth):
        w_in_t = jnp.swapaxes(w_in[l], 0, 1)
        b_f_pad = jnp.pad(b_f[l], (0, LANES - N_HEADS)).reshape(1, LANES)
        tn = PROJ_TN
        per = WIDTH // tn

        h = _rmsnorm(x2, g_mix[l], BF16)
        kk = _matmul_wt(h, w_in_t, lambda j: (j + jnp.where(j < per, per, 3 * per)) * tn,
                        2 * WIDTH, BF16, 1024, tn, "proj_k")
        qvt = _matmul_wt(h, w_in_t,
                         lambda j: (j + jnp.where(j < per, 0, jnp.where(j < 3 * per, per, 2 * per))) * tn,
                         4 * WIDTH, BF16, 1024, tn, "proj_qv", transpose_out=True)
        f_pad = _matmul_wt(h, w_in_t, lambda j: qkv_w + j * LANES, LANES, F32, 1024, LANES,
                           "proj_forget")
        gates = _matmul_wt(h, w_in_t, lambda j: qkv_w + N_HEADS + j * tn, 2 * d, F32, 1024, tn,
                           "proj_gates")

        c, ct = _forget_scan(f_pad, b_f_pad, batch, seq)
        ct = ct[:, :N_HEADS, :].reshape(batch * N_HEADS, 1, seq)
        cb = _forget_lane_bcast(c, batch, seq)

        o_a = _attn_a(qvt, kk, _relpos_offset_vector(rel_bias[l]), batch, seq)
        o_b = _fox(qvt, kk, cb, ct, batch, seq)

        merged = _merge(o_a, o_b, w_branch_a[l], w_branch_b[l], gates, b_gate[l])
        x2 = _matmul_residual(merged, w_out[l], x2, 1024, 512, "out_proj")

        h2 = _rmsnorm(x2, g_ffn[l], BF16)
        hidden = _swiglu_up(h2, w_gate_ffn[l], w_up_ffn[l])
        x2 = _matmul_residual_bf16(hidden, w_down_ffn[l].astype(BF16), x2, 512, 512, "ffn_down")
    return _rmsnorm(x2, g_final, F32).reshape(batch, seq, d)
```

```python
import functools
import math

import jax
import jax.numpy as jnp
from jax import lax
from jax.experimental import pallas as pl
from jax.experimental.pallas import tpu as pltpu

F32 = jnp.float32
BF16 = jnp.bfloat16

CHUNK = 64
LEFT_CHUNKS = 8
HEAD_DIM = 128
N_HEADS = 16
WIDTH = N_HEADS * HEAD_DIM
MAX_REL = 256
RMS_EPS = 1e-6
NEG_INF = -1e30
SCALE = HEAD_DIM ** -0.5
LOG2E = math.log2(math.e)

LANES = 128
VMEM_LIMIT = 56 * 1024 * 1024

HEADS_PER_STEP = 2
TQ_A = 256
KB_A = 3 * TQ_A
E_LEN = 4 * TQ_A
T_B = 512
PROJ_TN = 1024


def _params(*sem):
    return pltpu.CompilerParams(dimension_semantics=sem, vmem_limit_bytes=VMEM_LIMIT)


def _rmsnorm_kernel(x_ref, g_ref, o_ref):
    x = x_ref[...]
    y = x * lax.rsqrt(jnp.mean(x * x, axis=-1, keepdims=True) + RMS_EPS)
    o_ref[...] = (y * g_ref[...]).astype(o_ref.dtype)


def _rmsnorm(x, g, out_dtype, tm=256):
    m, d = x.shape
    return pl.pallas_call(
        _rmsnorm_kernel,
        grid=(m // tm,),
        in_specs=[pl.BlockSpec((tm, d), lambda i: (i, 0)),
                  pl.BlockSpec((1, d), lambda i: (0, 0))],
        out_specs=pl.BlockSpec((tm, d), lambda i: (i, 0)),
        out_shape=jax.ShapeDtypeStruct((m, d), out_dtype),
        compiler_params=_params("parallel"),
        name="rmsnorm",
    )(x, g.reshape(1, d))


def _cast_weights(pairs):
    @pl.when(pl.program_id(1) == 0)
    def _():
        for w_ref, wq_ref in pairs:
            wq_ref[...] = w_ref[...].astype(BF16)


_NT = (((1,), (1,)), ((), ()))


def _stage_weight_tiles(tile_copies, stage_refs, wq_refs):
    j = pl.program_id(0)

    @pl.when(pl.program_id(1) == 0)
    def _():
        @pl.when(j == 0)
        def _():
            for cp in tile_copies(j):
                cp.start()

        for cp in tile_copies(j):
            cp.wait()
        for stage_ref, wq_ref in zip(stage_refs, wq_refs):
            wq_ref[...] = stage_ref[...].astype(BF16)

        @pl.when(j + 1 < pl.num_programs(0))
        def _():
            for cp in tile_copies(j + 1):
                cp.start()


def _mm_wt_kernel(a_ref, wt_hbm, o_ref, stage_ref, wq_ref, sem, *, row_of, tn, transpose_out):
    def tile_copies(jj):
        rows = pl.ds(pl.multiple_of(row_of(jj), 8), tn)
        return [pltpu.make_async_copy(wt_hbm.at[rows, :], stage_ref, sem.at[0])]

    _stage_weight_tiles(tile_copies, [stage_ref], [wq_ref])
    if transpose_out:
        acc = lax.dot_general(wq_ref[...], a_ref[...], _NT, preferred_element_type=F32)
    else:
        acc = lax.dot_general(a_ref[...], wq_ref[...], _NT, preferred_element_type=F32)
    o_ref[...] = acc.astype(o_ref.dtype)


def _matmul_wt(a, wt, row_of, n, out_dtype, tm, tn, name, transpose_out=False):
    m, k = a.shape
    if transpose_out:
        out_spec = pl.BlockSpec((tn, tm), lambda j, i: (j, i))
        out_shape = jax.ShapeDtypeStruct((n, m), out_dtype)
    else:
        out_spec = pl.BlockSpec((tm, tn), lambda j, i: (i, j))
        out_shape = jax.ShapeDtypeStruct((m, n), out_dtype)
    return pl.pallas_call(
        functools.partial(_mm_wt_kernel, row_of=row_of, tn=tn, transpose_out=transpose_out),
        grid=(n // tn, m // tm),
        in_specs=[pl.BlockSpec((tm, k), lambda j, i: (i, 0)),
                  pl.BlockSpec(memory_space=pl.ANY)],
        out_specs=out_spec,
        out_shape=out_shape,
        scratch_shapes=[pltpu.VMEM((tn, k), F32), pltpu.VMEM((tn, k), BF16),
                        pltpu.SemaphoreType.DMA((1,))],
        compiler_params=_params("arbitrary", "arbitrary"),
        name=name,
    )(a, wt)


def _mm_res_kernel(a_ref, b_ref, r_ref, o_ref, bq_ref):
    _cast_weights([(b_ref, bq_ref)])
    o_ref[...] = r_ref[...] + jnp.dot(a_ref[...], bq_ref[...], preferred_element_type=F32)


def _matmul_residual(a, w, r, tm, tn, name):
    m, k = a.shape
    n = w.shape[1]
    return pl.pallas_call(
        _mm_res_kernel,
        grid=(n // tn, m // tm),
        in_specs=[pl.BlockSpec((tm, k), lambda j, i: (i, 0)),
                  pl.BlockSpec((k, tn), lambda j, i: (0, j)),
                  pl.BlockSpec((tm, tn), lambda j, i: (i, j))],
        out_specs=pl.BlockSpec((tm, tn), lambda j, i: (i, j)),
        out_shape=jax.ShapeDtypeStruct((m, n), F32),
        scratch_shapes=[pltpu.VMEM((k, tn), BF16)],
        compiler_params=_params("arbitrary", "arbitrary"),
        name=name,
    )(a, w, r)


def _mm_res_bf16_kernel(a_ref, b_ref, r_ref, o_ref):
    o_ref[...] = r_ref[...] + jnp.dot(a_ref[...], b_ref[...], preferred_element_type=F32)


def _matmul_residual_bf16(a, b, r, tm, tn, name):
    m, k = a.shape
    n = b.shape[1]
    return pl.pallas_call(
        _mm_res_bf16_kernel,
        grid=(m // tm, n // tn),
        in_specs=[pl.BlockSpec((tm, k), lambda i, j: (i, 0)),
                  pl.BlockSpec((k, tn), lambda i, j: (0, j)),
                  pl.BlockSpec((tm, tn), lambda i, j: (i, j))],
        out_specs=pl.BlockSpec((tm, tn), lambda i, j: (i, j)),
        out_shape=jax.ShapeDtypeStruct((m, n), F32),
        compiler_params=_params("parallel", "parallel"),
        name=name,
    )(a, b, r)


def _merge_kernel(oa_ref, ob_ref, wa_ref, wb_ref, ga_ref, gb_ref, ba_ref, bb_ref, o_ref,
                  waq_ref, wbq_ref):
    _cast_weights([(wa_ref, waq_ref), (wb_ref, wbq_ref)])
    ua = jnp.dot(oa_ref[...], waq_ref[...], preferred_element_type=F32)
    ub = jnp.dot(ob_ref[...], wbq_ref[...], preferred_element_type=F32)
    merged = (jax.nn.sigmoid(ga_ref[...] + ba_ref[...]) * ua
              + jax.nn.sigmoid(gb_ref[...] + bb_ref[...]) * ub)
    o_ref[...] = merged.astype(o_ref.dtype)


def _merge(o_a, o_b, w_a, w_b, gates, b_gate, tm=1024, tn=512):
    m, k = o_a.shape
    d = w_a.shape[1]
    nb = d // tn
    bg = b_gate.reshape(1, 2 * d)
    return pl.pallas_call(
        _merge_kernel,
        grid=(nb, m // tm),
        in_specs=[pl.BlockSpec((tm, k), lambda j, i: (i, 0)),
                  pl.BlockSpec((tm, k), lambda j, i: (i, 0)),
                  pl.BlockSpec((k, tn), lambda j, i: (0, j)),
                  pl.BlockSpec((k, tn), lambda j, i: (0, j)),
                  pl.BlockSpec((tm, tn), lambda j, i: (i, j)),
                  pl.BlockSpec((tm, tn), lambda j, i: (i, j + nb)),
                  pl.BlockSpec((1, tn), lambda j, i: (0, j)),
                  pl.BlockSpec((1, tn), lambda j, i: (0, j + nb))],
        out_specs=pl.BlockSpec((tm, tn), lambda j, i: (i, j)),
        out_shape=jax.ShapeDtypeStruct((m, d), BF16),
        scratch_shapes=[pltpu.VMEM((k, tn), BF16), pltpu.VMEM((k, tn), BF16)],
        compiler_params=_params("arbitrary", "arbitrary"),
        name="merge",
    )(o_a, o_b, w_a, w_b, gates, gates, bg, bg)


def _swiglu_up_kernel(a_ref, wg_hbm, wu_hbm, o_ref, sg_ref, su_ref, wgq_ref, wuq_ref, sem, *, tn):
    def tile_copies(jj):
        cols = pl.ds(pl.multiple_of(jj * tn, tn), tn)
        return [pltpu.make_async_copy(wg_hbm.at[:, cols], sg_ref, sem.at[0]),
                pltpu.make_async_copy(wu_hbm.at[:, cols], su_ref, sem.at[1])]

    _stage_weight_tiles(tile_copies, [sg_ref, su_ref], [wgq_ref, wuq_ref])
    a = a_ref[...]
    g = jnp.dot(a, wgq_ref[...], preferred_element_type=F32)
    u = jnp.dot(a, wuq_ref[...], preferred_element_type=F32)
    o_ref[...] = (g * jax.nn.sigmoid(g) * u).astype(o_ref.dtype)


def _swiglu_up(a, wg, wu, tm=2048, tn=256):
    m, k = a.shape
    f = wg.shape[1]
    return pl.pallas_call(
        functools.partial(_swiglu_up_kernel, tn=tn),
        grid=(f // tn, m // tm),
        in_specs=[pl.BlockSpec((tm, k), lambda j, i: (i, 0)),
                  pl.BlockSpec(memory_space=pl.ANY),
                  pl.BlockSpec(memory_space=pl.ANY)],
        out_specs=pl.BlockSpec((tm, tn), lambda j, i: (i, j)),
        out_shape=jax.ShapeDtypeStruct((m, f), BF16),
        scratch_shapes=[pltpu.VMEM((k, tn), F32), pltpu.VMEM((k, tn), F32),
                        pltpu.VMEM((k, tn), BF16), pltpu.VMEM((k, tn), BF16),
                        pltpu.SemaphoreType.DMA((2,))],
        compiler_params=_params("arbitrary", "arbitrary"),
        name="swiglu_up",
    )(a, wg, wu)


def _cum_kernel(f_ref, b_ref, c_ref, ct_ref):
    x = f_ref[...] + b_ref[...]
    c = jnp.minimum(x, 0.0) - jnp.log1p(jnp.exp(-jnp.abs(x)))
    n = c.shape[0]
    row = lax.broadcasted_iota(jnp.int32, c.shape, 0)
    shift = 1
    while shift < n:
        c = c + jnp.where(row >= shift, pltpu.roll(c, shift, axis=0), 0.0)
        shift *= 2
    c = c * LOG2E
    c_ref[...] = c
    ct_ref[...] = c.T


def _forget_scan(f_pad, b_pad, batch, seq):
    return pl.pallas_call(
        _cum_kernel,
        grid=(batch,),
        in_specs=[pl.BlockSpec((seq, LANES), lambda b: (b, 0)),
                  pl.BlockSpec((1, LANES), lambda b: (0, 0))],
        out_specs=[pl.BlockSpec((seq, LANES), lambda b: (b, 0)),
                   pl.BlockSpec((None, LANES, seq), lambda b: (b, 0, 0))],
        out_shape=[jax.ShapeDtypeStruct((batch * seq, LANES), F32),
                   jax.ShapeDtypeStruct((batch, LANES, seq), F32)],
        compiler_params=_params("parallel"),
        name="forget_scan",
    )(f_pad, b_pad)


def _lane_bcast_kernel(c_ref, o_ref):
    c = c_ref[...]
    for hh in range(N_HEADS):
        o_ref[hh] = jnp.broadcast_to(c[:, hh:hh + 1], c.shape)


def _forget_lane_bcast(c, batch, seq, tb=512):
    nb = seq // tb
    return pl.pallas_call(
        _lane_bcast_kernel,
        grid=(batch, nb),
        in_specs=[pl.BlockSpec((tb, LANES), lambda b, t: (b * nb + t, 0))],
        out_specs=pl.BlockSpec((N_HEADS, tb, LANES), lambda b, t: (b, t, 0)),
        out_shape=jax.ShapeDtypeStruct((batch * N_HEADS, seq, LANES), F32),
        compiler_params=_params("parallel", "parallel"),
        name="forget_lane_bcast",
    )(c)


def _head_rows(u):
    return slice(u * HEAD_DIM, (u + 1) * HEAD_DIM)


def _attn_a_kernel(e_ref, qt_ref, k_ref, vt_ref, o_ref, bias_ref):
    qi = pl.program_id(2)

    @pl.when(qi == 0)
    def _():
        qc = lax.broadcasted_iota(jnp.int32, (TQ_A, KB_A), 0) // CHUNK
        kc = lax.broadcasted_iota(jnp.int32, (TQ_A, KB_A), 1) // CHUNK
        valid = (kc >= qc) & (kc <= qc + LEFT_CHUNKS)
        for u in range(HEADS_PER_STEP):
            e = jnp.broadcast_to(e_ref[u], (TQ_A, E_LEN))
            t = pltpu.roll(e, 0, 1, stride=1, stride_axis=0)[:, E_LEN - KB_A:]
            bias_ref[u] = (jnp.where(valid, t, NEG_INF) * LOG2E).T

    starts = [pl.multiple_of(jnp.maximum(qi - back, 0) * TQ_A, TQ_A) for back in (2, 1, 0)]
    pens = [jnp.where(qi >= 2, 0.0, NEG_INF), jnp.where(qi >= 1, 0.0, NEG_INF), None]
    heads = [_head_rows(u) for u in range(HEADS_PER_STEP)]
    scores = []
    for u, hd in enumerate(heads):
        qt = qt_ref[hd, :]
        parts = []
        for blk in range(3):
            k = k_ref[pl.ds(starts[blk], TQ_A), hd]
            s = jnp.dot(k, qt, preferred_element_type=F32)
            s = s * (SCALE * LOG2E) + bias_ref[u, blk * TQ_A:(blk + 1) * TQ_A, :]
            if pens[blk] is not None:
                s = s + pens[blk]
            parts.append(s)
        scores.append(jnp.concatenate(parts, axis=0))
    probs = []
    for s in scores:
        m = jnp.max(s, axis=0, keepdims=True)
        p = jnp.exp2(s - m)
        probs.append((p.astype(BF16), jnp.sum(p, axis=0, keepdims=True)))
    for hd, (p, l) in zip(heads, probs):
        vt = jnp.concatenate([vt_ref[hd, pl.ds(st, TQ_A)] for st in starts], axis=1)
        ot = jnp.dot(vt, p, preferred_element_type=F32)
        o_ref[:, hd] = (ot / l).T.astype(o_ref.dtype)


def _attn_a(qvt, kk, e_tab, batch, seq):
    nq = seq // TQ_A
    hb = HEADS_PER_STEP
    w = hb * HEAD_DIM
    ng = N_HEADS // hb
    return pl.pallas_call(
        _attn_a_kernel,
        grid=(batch, ng, nq),
        in_specs=[pl.BlockSpec((hb, 1, E_LEN), lambda b, g, qi: (g, 0, 0)),
                  pl.BlockSpec((w, TQ_A), lambda b, g, qi: (g, b * nq + qi)),
                  pl.BlockSpec((seq, w), lambda b, g, qi: (b, g)),
                  pl.BlockSpec((w, seq), lambda b, g, qi: (ng + g, b))],
        out_specs=pl.BlockSpec((TQ_A, w), lambda b, g, qi: (b * nq + qi, g)),
        out_shape=jax.ShapeDtypeStruct((batch * seq, WIDTH), BF16),
        scratch_shapes=[pltpu.VMEM((hb, KB_A, TQ_A), F32)],
        compiler_params=_params("arbitrary", "arbitrary", "arbitrary"),
        name="attn_a",
    )(e_tab, qvt, kk, qvt)


def _fox_kernel(qt_ref, k_ref, vt_ref, cb_ref, ct_ref, o_ref):
    qi = pl.program_id(2)
    hb = HEADS_PER_STEP
    q0 = pl.multiple_of(qi * T_B, T_B)

    def step(j, carry, masked):
        start = pl.multiple_of(j * T_B, T_B)
        heads = [_head_rows(u) for u in range(hb)]
        logits = []
        for u, hd in enumerate(heads):
            a = jnp.dot(k_ref[pl.ds(start, T_B), hd], qt_ref[hd, :],
                        preferred_element_type=F32)
            cb = cb_ref[u, pl.ds(start, T_B), :]
            a = a * (SCALE * LOG2E) - jnp.concatenate([cb] * (T_B // LANES), axis=1)
            if masked:
                r = lax.broadcasted_iota(jnp.int32, a.shape, 0)
                c = lax.broadcasted_iota(jnp.int32, a.shape, 1)
                a = jnp.where(r <= c, a, NEG_INF)
            logits.append(a)
        stats = []
        for u, a in enumerate(logits):
            m, l, _ = carry[u]
            cq = ct_ref[u, :, pl.ds(q0, T_B)]
            m_new = jnp.maximum(m, jnp.max(a, axis=0, keepdims=True) + cq)
            alpha = jnp.exp2(m - m_new)
            p = jnp.exp2(a + (cq - m_new))
            l = alpha * l + jnp.sum(p, axis=0, keepdims=True)
            stats.append((m_new, l, alpha, p.astype(BF16)))
        out = []
        for u, hd in enumerate(heads):
            m_new, l, alpha, p = stats[u]
            pv = jnp.dot(vt_ref[hd, pl.ds(start, T_B)], p,
                         preferred_element_type=F32)
            out.append((m_new, l, alpha * carry[u][2] + pv))
        return tuple(out)

    init = tuple((jnp.full((1, T_B), NEG_INF, F32), jnp.zeros((1, T_B), F32),
                  jnp.zeros((HEAD_DIM, T_B), F32)) for _ in range(hb))
    carry = lax.fori_loop(0, qi, functools.partial(step, masked=False), init)
    carry = step(qi, carry, masked=True)
    for u in range(hb):
        _, l, acc = carry[u]
        o_ref[:, _head_rows(u)] = (acc / l).T.astype(o_ref.dtype)


def _fox(qvt, kk, cb, ct, batch, seq):
    nq = seq // T_B
    hb = HEADS_PER_STEP
    w = hb * HEAD_DIM
    ng = N_HEADS // hb
    return pl.pallas_call(
        _fox_kernel,
        grid=(batch, ng, nq),
        in_specs=[pl.BlockSpec((w, T_B), lambda b, g, qi: (2 * ng + g, b * nq + qi)),
                  pl.BlockSpec((seq, w), lambda b, g, qi: (b, ng + g)),
                  pl.BlockSpec((w, seq), lambda b, g, qi: (3 * ng + g, b)),
                  pl.BlockSpec((hb, seq, LANES), lambda b, g, qi: (b * ng + g, 0, 0)),
                  pl.BlockSpec((hb, 1, seq), lambda b, g, qi: (b * ng + g, 0, 0))],
        out_specs=pl.BlockSpec((T_B, w), lambda b, g, qi: (b * nq + qi, g)),
        out_shape=jax.ShapeDtypeStruct((batch * seq, WIDTH), BF16),
        compiler_params=_params("parallel", "parallel", "parallel"),
        name="fox",
    )(qvt, kk, qvt, cb, ct)


def _relpos_offset_vector(rel_bias):
    far = jnp.broadcast_to(rel_bias[:, 2 * MAX_REL:], (rel_bias.shape[0], E_LEN - 2 * MAX_REL))
    near = rel_bias[:, :0:-1]
    return jnp.concatenate([far, near], axis=1).astype(F32)[:, None, :]


def kernel(x, g_mix, w_in, b_f, b_gate, rel_bias, w_branch_a, w_branch_b, w_out,
           g_ffn, w_gate_ffn, w_up_ffn, w_down_ffn, g_final):
    batch, seq, d = x.shape
    m = batch * seq
    depth = g_mix.shape[0]
    qkv_w = 6 * WIDTH
    x2 = x.reshape(m, d)
    for l in range(depth):
        w_in_t = jnp.swapaxes(w_in[l], 0, 1)
        b_f_pad = jnp.pad(b_f[l], (0, LANES - N_HEADS)).reshape(1, LANES)
        tn = PROJ_TN
        per = WIDTH // tn

        h = _rmsnorm(x2, g_mix[l], BF16)
        kk = _matmul_wt(h, w_in_t, lambda j: (j + jnp.where(j < per, per, 3 * per)) * tn,
                        2 * WIDTH, BF16, 1024, tn, "proj_k")
        qvt = _matmul_wt(h, w_in_t,
                         lambda j: (j + jnp.where(j < per, 0, jnp.where(j < 3 * per, per, 2 * per))) * tn,
                         4 * WIDTH, BF16, 1024, tn, "proj_qv", transpose_out=True)
        f_pad = _matmul_wt(h, w_in_t, lambda j: qkv_w + j * LANES, LANES, F32, 1024, LANES,
                           "proj_forget")
        gates = _matmul_wt(h, w_in_t, lambda j: qkv_w + N_HEADS + j * tn, 2 * d, F32, 1024, tn,
                           "proj_gates")

        c, ct = _forget_scan(f_pad, b_f_pad, batch, seq)
        ct = ct[:, :N_HEADS, :].reshape(batch * N_HEADS, 1, seq)
        cb = _forget_lane_bcast(c, batch, seq)

        o_a = _attn_a(qvt, kk, _relpos_offset_vector(rel_bias[l]), batch, seq)
        o_b = _fox(qvt, kk, cb, ct, batch, seq)

        merged = _merge(o_a, o_b, w_branch_a[l], w_branch_b[l], gates, b_gate[l])
        x2 = _matmul_residual(merged, w_out[l], x2, 1024, 512, "out_proj")

        h2 = _rmsnorm(x2, g_ffn[l], BF16)
        hidden = _swiglu_up(h2, w_gate_ffn[l], w_up_ffn[l])
        x2 = _matmul_residual_bf16(hidden, w_down_ffn[l].astype(BF16), x2, 512, 512, "ffn_down")
    return _rmsnorm(x2, g_final, F32).reshape(batch, seq, d)
```

```python
import functools
import math

import jax
import jax.numpy as jnp
from jax import lax
from jax.experimental import pallas as pl
from jax.experimental.pallas import tpu as pltpu

F32 = jnp.float32
BF16 = jnp.bfloat16

CHUNK = 64
LEFT_CHUNKS = 8
HEAD_DIM = 128
N_HEADS = 16
WIDTH = N_HEADS * HEAD_DIM
MAX_REL = 256
RMS_EPS = 1e-6
NEG_INF = -1e30
SCALE = HEAD_DIM ** -0.5
LOG2E = math.log2(math.e)

LANES = 128
VMEM_LIMIT = 56 * 1024 * 1024

HEADS_A = 8
HEADS_B = 4
TQ_A = 256
KB_A = 3 * TQ_A
E_LEN = 4 * TQ_A
T_B = 512
PROJ_TN = 1024


def _params(*sem):
    return pltpu.CompilerParams(dimension_semantics=sem, vmem_limit_bytes=VMEM_LIMIT)


def _rmsnorm_kernel(x_ref, g_ref, o_ref):
    x = x_ref[...]
    y = x * lax.rsqrt(jnp.mean(x * x, axis=-1, keepdims=True) + RMS_EPS)
    o_ref[...] = (y * g_ref[...]).astype(o_ref.dtype)


def _rmsnorm(x, g, out_dtype, tm=256):
    m, d = x.shape
    return pl.pallas_call(
        _rmsnorm_kernel,
        grid=(m // tm,),
        in_specs=[pl.BlockSpec((tm, d), lambda i: (i, 0)),
                  pl.BlockSpec((1, d), lambda i: (0, 0))],
        out_specs=pl.BlockSpec((tm, d), lambda i: (i, 0)),
        out_shape=jax.ShapeDtypeStruct((m, d), out_dtype),
        compiler_params=_params("parallel"),
        name="rmsnorm",
    )(x, g.reshape(1, d))


def _cast_weights(pairs):
    @pl.when(pl.program_id(1) == 0)
    def _():
        for w_ref, wq_ref in pairs:
            wq_ref[...] = w_ref[...].astype(BF16)


_NT = (((1,), (1,)), ((), ()))


def _stage_weight_tiles(tile_copies, stage_refs, wq_refs):
    j = pl.program_id(0)

    @pl.when(pl.program_id(1) == 0)
    def _():
        @pl.when(j == 0)
        def _():
            for cp in tile_copies(j):
                cp.start()

        for cp in tile_copies(j):
            cp.wait()
        for stage_ref, wq_ref in zip(stage_refs, wq_refs):
            wq_ref[...] = stage_ref[...].astype(BF16)

        @pl.when(j + 1 < pl.num_programs(0))
        def _():
            for cp in tile_copies(j + 1):
                cp.start()


def _mm_wt_kernel(a_ref, wt_hbm, o_ref, stage_ref, wq_ref, sem, *, row_of, tn, transpose_out):
    def tile_copies(jj):
        rows = pl.ds(pl.multiple_of(row_of(jj), 8), tn)
        return [pltpu.make_async_copy(wt_hbm.at[rows, :], stage_ref, sem.at[0])]

    _stage_weight_tiles(tile_copies, [stage_ref], [wq_ref])
    if transpose_out:
        acc = lax.dot_general(wq_ref[...], a_ref[...], _NT, preferred_element_type=F32)
    else:
        acc = lax.dot_general(a_ref[...], wq_ref[...], _NT, preferred_element_type=F32)
    o_ref[...] = acc.astype(o_ref.dtype)


def _matmul_wt(a, wt, row_of, n, out_dtype, tm, tn, name, transpose_out=False):
    m, k = a.shape
    if transpose_out:
        out_spec = pl.BlockSpec((tn, tm), lambda j, i: (j, i))
        out_shape = jax.ShapeDtypeStruct((n, m), out_dtype)
    else:
        out_spec = pl.BlockSpec((tm, tn), lambda j, i: (i, j))
        out_shape = jax.ShapeDtypeStruct((m, n), out_dtype)
    return pl.pallas_call(
        functools.partial(_mm_wt_kernel, row_of=row_of, tn=tn, transpose_out=transpose_out),
        grid=(n // tn, m // tm),
        in_specs=[pl.BlockSpec((tm, k), lambda j, i: (i, 0)),
                  pl.BlockSpec(memory_space=pl.ANY)],
        out_specs=out_spec,
        out_shape=out_shape,
        scratch_shapes=[pltpu.VMEM((tn, k), F32), pltpu.VMEM((tn, k), BF16),
                        pltpu.SemaphoreType.DMA((1,))],
        compiler_params=_params("arbitrary", "arbitrary"),
        name=name,
    )(a, wt)


def _mm_res_kernel(a_ref, w_hbm, r_ref, o_ref, stage_ref, wq_ref, sem, *, k0, tn):
    def tile_copies(jj):
        rows = pl.ds(k0, stage_ref.shape[0])
        cols = pl.ds(pl.multiple_of(jj * tn, tn), tn)
        return [pltpu.make_async_copy(w_hbm.at[rows, cols], stage_ref, sem.at[0])]

    _stage_weight_tiles(tile_copies, [stage_ref], [wq_ref])
    o_ref[...] = r_ref[...] + jnp.dot(a_ref[...], wq_ref[...], preferred_element_type=F32)


def _matmul_residual(a, w, r, tm, tn, name, k_slab=0, n_slabs=1):
    m = a.shape[0]
    k, n = w.shape
    kk = k // n_slabs
    return pl.pallas_call(
        functools.partial(_mm_res_kernel, k0=k_slab * kk, tn=tn),
        grid=(n // tn, m // tm),
        in_specs=[pl.BlockSpec((tm, kk), lambda j, i: (i, k_slab)),
                  pl.BlockSpec(memory_space=pl.ANY),
                  pl.BlockSpec((tm, tn), lambda j, i: (i, j))],
        out_specs=pl.BlockSpec((tm, tn), lambda j, i: (i, j)),
        out_shape=jax.ShapeDtypeStruct((m, n), F32),
        scratch_shapes=[pltpu.VMEM((kk, tn), F32), pltpu.VMEM((kk, tn), BF16),
                        pltpu.SemaphoreType.DMA((1,))],
        compiler_params=_params("arbitrary", "arbitrary"),
        name=name,
    )(a, w, r)


def _merge_kernel(oa_ref, ob_ref, wa_ref, wb_ref, ga_ref, gb_ref, ba_ref, bb_ref, o_ref,
                  waq_ref, wbq_ref):
    _cast_weights([(wa_ref, waq_ref), (wb_ref, wbq_ref)])
    ua = jnp.dot(oa_ref[...], waq_ref[...], preferred_element_type=F32)
    ub = jnp.dot(ob_ref[...], wbq_ref[...], preferred_element_type=F32)
    merged = (jax.nn.sigmoid(ga_ref[...] + ba_ref[...]) * ua
              + jax.nn.sigmoid(gb_ref[...] + bb_ref[...]) * ub)
    o_ref[...] = merged.astype(o_ref.dtype)


def _merge(o_a, o_b, w_a, w_b, gates, b_gate, tm=1024, tn=512):
    m, k = o_a.shape
    d = w_a.shape[1]
    nb = d // tn
    bg = b_gate.reshape(1, 2 * d)
    return pl.pallas_call(
        _merge_kernel,
        grid=(nb, m // tm),
        in_specs=[pl.BlockSpec((tm, k), lambda j, i: (i, 0)),
                  pl.BlockSpec((tm, k), lambda j, i: (i, 0)),
                  pl.BlockSpec((k, tn), lambda j, i: (0, j)),
                  pl.BlockSpec((k, tn), lambda j, i: (0, j)),
                  pl.BlockSpec((tm, tn), lambda j, i: (i, j)),
                  pl.BlockSpec((tm, tn), lambda j, i: (i, j + nb)),
                  pl.BlockSpec((1, tn), lambda j, i: (0, j)),
                  pl.BlockSpec((1, tn), lambda j, i: (0, j + nb))],
        out_specs=pl.BlockSpec((tm, tn), lambda j, i: (i, j)),
        out_shape=jax.ShapeDtypeStruct((m, d), BF16),
        scratch_shapes=[pltpu.VMEM((k, tn), BF16), pltpu.VMEM((k, tn), BF16)],
        compiler_params=_params("arbitrary", "arbitrary"),
        name="merge",
    )(o_a, o_b, w_a, w_b, gates, gates, bg, bg)


def _swiglu_up_kernel(a_ref, wg_hbm, wu_hbm, o_ref, sg_ref, su_ref, wgq_ref, wuq_ref, sem, *, tn):
    def tile_copies(jj):
        cols = pl.ds(pl.multiple_of(jj * tn, tn), tn)
        return [pltpu.make_async_copy(wg_hbm.at[:, cols], sg_ref, sem.at[0]),
                pltpu.make_async_copy(wu_hbm.at[:, cols], su_ref, sem.at[1])]

    _stage_weight_tiles(tile_copies, [sg_ref, su_ref], [wgq_ref, wuq_ref])
    a = a_ref[...]
    g = jnp.dot(a, wgq_ref[...], preferred_element_type=F32)
    u = jnp.dot(a, wuq_ref[...], preferred_element_type=F32)
    o_ref[...] = (g * jax.nn.sigmoid(g) * u).astype(o_ref.dtype)


def _swiglu_up(a, wg, wu, tm=2048, tn=256):
    m, k = a.shape
    f = wg.shape[1]
    return pl.pallas_call(
        functools.partial(_swiglu_up_kernel, tn=tn),
        grid=(f // tn, m // tm),
        in_specs=[pl.BlockSpec((tm, k), lambda j, i: (i, 0)),
                  pl.BlockSpec(memory_space=pl.ANY),
                  pl.BlockSpec(memory_space=pl.ANY)],
        out_specs=pl.BlockSpec((tm, tn), lambda j, i: (i, j)),
        out_shape=jax.ShapeDtypeStruct((m, f), BF16),
        scratch_shapes=[pltpu.VMEM((k, tn), F32), pltpu.VMEM((k, tn), F32),
                        pltpu.VMEM((k, tn), BF16), pltpu.VMEM((k, tn), BF16),
                        pltpu.SemaphoreType.DMA((2,))],
        compiler_params=_params("arbitrary", "arbitrary"),
        name="swiglu_up",
    )(a, wg, wu)


def _cum_kernel(f_ref, b_ref, c_ref, ct_ref):
    x = f_ref[...] + b_ref[...]
    c = jnp.minimum(x, 0.0) - jnp.log1p(jnp.exp(-jnp.abs(x)))
    n = c.shape[0]
    row = lax.broadcasted_iota(jnp.int32, c.shape, 0)
    shift = 1
    while shift < n:
        c = c + jnp.where(row >= shift, pltpu.roll(c, shift, axis=0), 0.0)
        shift *= 2
    c = c * LOG2E
    c_ref[...] = c
    ct_ref[...] = c.T


def _forget_scan(f_pad, b_pad, batch, seq):
    return pl.pallas_call(
        _cum_kernel,
        grid=(batch,),
        in_specs=[pl.BlockSpec((seq, LANES), lambda b: (b, 0)),
                  pl.BlockSpec((1, LANES), lambda b: (0, 0))],
        out_specs=[pl.BlockSpec((seq, LANES), lambda b: (b, 0)),
                   pl.BlockSpec((None, LANES, seq), lambda b: (b, 0, 0))],
        out_shape=[jax.ShapeDtypeStruct((batch * seq, LANES), F32),
                   jax.ShapeDtypeStruct((batch, LANES, seq), F32)],
        compiler_params=_params("parallel"),
        name="forget_scan",
    )(f_pad, b_pad)


def _lane_bcast_kernel(c_ref, o_ref):
    c = c_ref[...]
    for hh in range(N_HEADS):
        o_ref[hh] = jnp.broadcast_to(c[:, hh:hh + 1], c.shape)


def _forget_lane_bcast(c, batch, seq, tb=512):
    nb = seq // tb
    return pl.pallas_call(
        _lane_bcast_kernel,
        grid=(batch, nb),
        in_specs=[pl.BlockSpec((tb, LANES), lambda b, t: (b * nb + t, 0))],
        out_specs=pl.BlockSpec((N_HEADS, tb, LANES), lambda b, t: (b, t, 0)),
        out_shape=jax.ShapeDtypeStruct((batch * N_HEADS, seq, LANES), F32),
        compiler_params=_params("parallel", "parallel"),
        name="forget_lane_bcast",
    )(c)


def _head_rows(u):
    return slice(u * HEAD_DIM, (u + 1) * HEAD_DIM)


def _attn_a_kernel(e_ref, qt_ref, k_ref, vt_ref, o_ref, bias_ref):
    qi = pl.program_id(2)

    @pl.when(qi == 0)
    def _():
        qc = lax.broadcasted_iota(jnp.int32, (TQ_A, KB_A), 0) // CHUNK
        kc = lax.broadcasted_iota(jnp.int32, (TQ_A, KB_A), 1) // CHUNK
        valid = (kc >= qc) & (kc <= qc + LEFT_CHUNKS)
        for u in range(HEADS_A):
            e = jnp.broadcast_to(e_ref[u], (TQ_A, E_LEN))
            t = pltpu.roll(e, 0, 1, stride=1, stride_axis=0)[:, E_LEN - KB_A:]
            bias_ref[u] = (jnp.where(valid, t, NEG_INF) * LOG2E).T

    starts = [pl.multiple_of(jnp.maximum(qi - back, 0) * TQ_A, TQ_A) for back in (2, 1, 0)]
    pens = [jnp.where(qi >= 2, 0.0, NEG_INF), jnp.where(qi >= 1, 0.0, NEG_INF), None]
    heads = [_head_rows(u) for u in range(HEADS_A)]
    scores = []
    for u, hd in enumerate(heads):
        qt = qt_ref[hd, :]
        parts = []
        for blk in range(3):
            k = k_ref[pl.ds(starts[blk], TQ_A), hd]
            s = jnp.dot(k, qt, preferred_element_type=F32)
            s = s * (SCALE * LOG2E) + bias_ref[u, blk * TQ_A:(blk + 1) * TQ_A, :]
            if pens[blk] is not None:
                s = s + pens[blk]
            parts.append(s)
        scores.append(jnp.concatenate(parts, axis=0))
    probs = []
    for s in scores:
        m = jnp.max(s, axis=0, keepdims=True)
        p = jnp.exp2(s - m)
        probs.append((p.astype(BF16), jnp.sum(p, axis=0, keepdims=True)))
    for hd, (p, l) in zip(heads, probs):
        vt = jnp.concatenate([vt_ref[hd, pl.ds(st, TQ_A)] for st in starts], axis=1)
        ot = jnp.dot(vt, p, preferred_element_type=F32)
        o_ref[:, hd] = (ot / l).T.astype(o_ref.dtype)


def _attn_a(qvt, kk, e_tab, batch, seq):
    nq = seq // TQ_A
    hb = HEADS_A
    w = hb * HEAD_DIM
    ng = N_HEADS // hb
    return pl.pallas_call(
        _attn_a_kernel,
        grid=(batch, ng, nq),
        in_specs=[pl.BlockSpec((hb, 1, E_LEN), lambda b, g, qi: (g, 0, 0)),
                  pl.BlockSpec((w, TQ_A), lambda b, g, qi: (g, b * nq + qi)),
                  pl.BlockSpec((seq, w), lambda b, g, qi: (b, g)),
                  pl.BlockSpec((w, seq), lambda b, g, qi: (ng + g, b))],
        out_specs=pl.BlockSpec((TQ_A, w), lambda b, g, qi: (b * nq + qi, g)),
        out_shape=jax.ShapeDtypeStruct((batch * seq, WIDTH), BF16),
        scratch_shapes=[pltpu.VMEM((hb, KB_A, TQ_A), F32)],
        compiler_params=_params("arbitrary", "arbitrary", "arbitrary"),
        name="attn_a",
    )(e_tab, qvt, kk, qvt)


def _fox_kernel(qt_ref, k_ref, vt_ref, cb_ref, ct_ref, o_ref):
    qi = pl.program_id(2)
    hb = HEADS_B
    q0 = pl.multiple_of(qi * T_B, T_B)

    def step(j, carry, masked):
        start = pl.multiple_of(j * T_B, T_B)
        heads = [_head_rows(u) for u in range(hb)]
        logits = []
        for u, hd in enumerate(heads):
            a = jnp.dot(k_ref[pl.ds(start, T_B), hd], qt_ref[hd, :],
                        preferred_element_type=F32)
            cb = cb_ref[u, pl.ds(start, T_B), :]
            a = a * (SCALE * LOG2E) - jnp.concatenate([cb] * (T_B // LANES), axis=1)
            if masked:
                r = lax.broadcasted_iota(jnp.int32, a.shape, 0)
                c = lax.broadcasted_iota(jnp.int32, a.shape, 1)
                a = jnp.where(r <= c, a, NEG_INF)
            logits.append(a)
        stats = []
        for u, a in enumerate(logits):
            m, l, _ = carry[u]
            cq = ct_ref[u, :, pl.ds(q0, T_B)]
            m_new = jnp.maximum(m, jnp.max(a, axis=0, keepdims=True) + cq)
            alpha = jnp.exp2(m - m_new)
            p = jnp.exp2(a + (cq - m_new))
            l = alpha * l + jnp.sum(p, axis=0, keepdims=True)
            stats.append((m_new, l, alpha, p.astype(BF16)))
        out = []
        for u, hd in enumerate(heads):
            m_new, l, alpha, p = stats[u]
            pv = jnp.dot(vt_ref[hd, pl.ds(start, T_B)], p,
                         preferred_element_type=F32)
            out.append((m_new, l, alpha * carry[u][2] + pv))
        return tuple(out)

    init = tuple((jnp.full((1, T_B), NEG_INF, F32), jnp.zeros((1, T_B), F32),
                  jnp.zeros((HEAD_DIM, T_B), F32)) for _ in range(hb))
    carry = lax.fori_loop(0, qi, functools.partial(step, masked=False), init)
    carry = step(qi, carry, masked=True)
    for u in range(hb):
        _, l, acc = carry[u]
        o_ref[:, _head_rows(u)] = (acc / l).T.astype(o_ref.dtype)


def _fox(qvt, kk, cb, ct, batch, seq):
    nq = seq // T_B
    hb = HEADS_B
    w = hb * HEAD_DIM
    ng = N_HEADS // hb
    return pl.pallas_call(
        _fox_kernel,
        grid=(batch, ng, nq),
        in_specs=[pl.BlockSpec((w, T_B), lambda b, g, qi: (2 * ng + g, b * nq + qi)),
                  pl.BlockSpec((seq, w), lambda b, g, qi: (b, ng + g)),
                  pl.BlockSpec((w, seq), lambda b, g, qi: (3 * ng + g, b)),
                  pl.BlockSpec((hb, seq, LANES), lambda b, g, qi: (b * ng + g, 0, 0)),
                  pl.BlockSpec((hb, 1, seq), lambda b, g, qi: (b * ng + g, 0, 0))],
        out_specs=pl.BlockSpec((T_B, w), lambda b, g, qi: (b * nq + qi, g)),
        out_shape=jax.ShapeDtypeStruct((batch * seq, WIDTH), BF16),
        compiler_params=_params("parallel", "parallel", "parallel"),
        name="fox",
    )(qvt, kk, qvt, cb, ct)


def _relpos_offset_vector(rel_bias):
    far = jnp.broadcast_to(rel_bias[:, 2 * MAX_REL:], (rel_bias.shape[0], E_LEN - 2 * MAX_REL))
    near = rel_bias[:, :0:-1]
    return jnp.concatenate([far, near], axis=1).astype(F32)[:, None, :]


def kernel(x, g_mix, w_in, b_f, b_gate, rel_bias, w_branch_a, w_branch_b, w_out,
           g_ffn, w_gate_ffn, w_up_ffn, w_down_ffn, g_final):
    batch, seq, d = x.shape
    m = batch * seq
    depth = g_mix.shape[0]
    qkv_w = 6 * WIDTH
    x2 = x.reshape(m, d)
    for l in range(depth):
        w_in_t = jnp.swapaxes(w_in[l], 0, 1)
        b_f_pad = jnp.pad(b_f[l], (0, LANES - N_HEADS)).reshape(1, LANES)
        tn = PROJ_TN
        per = WIDTH // tn

        h = _rmsnorm(x2, g_mix[l], BF16)
        kk = _matmul_wt(h, w_in_t, lambda j: (j + jnp.where(j < per, per, 3 * per)) * tn,
                        2 * WIDTH, BF16, 1024, tn, "proj_k")
        qvt = _matmul_wt(h, w_in_t,
                         lambda j: (j + jnp.where(j < per, 0, jnp.where(j < 3 * per, per, 2 * per))) * tn,
                         4 * WIDTH, BF16, 1024, tn, "proj_qv", transpose_out=True)
        f_pad = _matmul_wt(h, w_in_t, lambda j: qkv_w + j * LANES, LANES, F32, 1024, LANES,
                           "proj_forget")
        gates = _matmul_wt(h, w_in_t, lambda j: qkv_w + N_HEADS + j * tn, 2 * d, F32, 1024, tn,
                           "proj_gates")

        c, ct = _forget_scan(f_pad, b_f_pad, batch, seq)
        ct = ct[:, :N_HEADS, :].reshape(batch * N_HEADS, 1, seq)
        cb = _forget_lane_bcast(c, batch, seq)

        o_a = _attn_a(qvt, kk, _relpos_offset_vector(rel_bias[l]), batch, seq)
        o_b = _fox(qvt, kk, cb, ct, batch, seq)

        merged = _merge(o_a, o_b, w_branch_a[l], w_branch_b[l], gates, b_gate[l])
        x2 = _matmul_residual(merged, w_out[l], x2, 1024, 512, "out_proj")

        h2 = _rmsnorm(x2, g_ffn[l], BF16)
        hidden = _swiglu_up(h2, w_gate_ffn[l], w_up_ffn[l])
        for slab in range(2):
            x2 = _matmul_residual(hidden, w_down_ffn[l], x2, 1024, 512, "ffn_down",
                                  k_slab=slab, n_slabs=2)
    return _rmsnorm(x2, g_final, F32).reshape(batch, seq, d)
```

```python
import functools
import math

import jax
import jax.numpy as jnp
from jax import lax
from jax.experimental import pallas as pl
from jax.experimental.pallas import tpu as pltpu

F32 = jnp.float32
BF16 = jnp.bfloat16

CHUNK = 64
LEFT_CHUNKS = 8
HEAD_DIM = 128
N_HEADS = 16
WIDTH = N_HEADS * HEAD_DIM
MAX_REL = 256
RMS_EPS = 1e-6
NEG_INF = -1e30
SCALE = HEAD_DIM ** -0.5
LOG2E = math.log2(math.e)

LANES = 128
VMEM_LIMIT = 56 * 1024 * 1024

HEADS_A = 8
HEADS_B = 4
TQ_A = 256
KB_A = 3 * TQ_A
E_LEN = 4 * TQ_A
T_B = 512
PROJ_TN = 1024


def _params(*sem):
    return pltpu.CompilerParams(dimension_semantics=sem, vmem_limit_bytes=VMEM_LIMIT)


def _rmsnorm_kernel(x_ref, g_ref, o_ref):
    x = x_ref[...]
    y = x * lax.rsqrt(jnp.mean(x * x, axis=-1, keepdims=True) + RMS_EPS)
    o_ref[...] = (y * g_ref[...]).astype(o_ref.dtype)


def _rmsnorm(x, g, out_dtype, tm=256):
    m, d = x.shape
    return pl.pallas_call(
        _rmsnorm_kernel,
        grid=(m // tm,),
        in_specs=[pl.BlockSpec((tm, d), lambda i: (i, 0)),
                  pl.BlockSpec((1, d), lambda i: (0, 0))],
        out_specs=pl.BlockSpec((tm, d), lambda i: (i, 0)),
        out_shape=jax.ShapeDtypeStruct((m, d), out_dtype),
        compiler_params=_params("parallel"),
        name="rmsnorm",
    )(x, g.reshape(1, d))


def _cast_weights(pairs):
    @pl.when(pl.program_id(1) == 0)
    def _():
        for w_ref, wq_ref in pairs:
            wq_ref[...] = w_ref[...].astype(BF16)


_NT = (((1,), (1,)), ((), ()))


def _stage_weight_tiles(tile_copies, stage_refs, wq_refs):
    j = pl.program_id(0)

    @pl.when(pl.program_id(1) == 0)
    def _():
        @pl.when(j == 0)
        def _():
            for cp in tile_copies(j):
                cp.start()

        for cp in tile_copies(j):
            cp.wait()
        for stage_ref, wq_ref in zip(stage_refs, wq_refs):
            wq_ref[...] = stage_ref[...].astype(BF16)

        @pl.when(j + 1 < pl.num_programs(0))
        def _():
            for cp in tile_copies(j + 1):
                cp.start()


def _mm_wt_kernel(a_ref, wt_hbm, o_ref, stage_ref, wq_ref, sem, *, row_of, tn, transpose_out,
                  scale_of):
    def tile_copies(jj):
        rows = pl.ds(pl.multiple_of(row_of(jj), 8), tn)
        return [pltpu.make_async_copy(wt_hbm.at[rows, :], stage_ref, sem.at[0])]

    _stage_weight_tiles(tile_copies, [stage_ref], [wq_ref])
    if transpose_out:
        acc = lax.dot_general(wq_ref[...], a_ref[...], _NT, preferred_element_type=F32)
    else:
        acc = lax.dot_general(a_ref[...], wq_ref[...], _NT, preferred_element_type=F32)
    if scale_of is not None:
        acc = acc * scale_of(pl.program_id(0))
    o_ref[...] = acc.astype(o_ref.dtype)


def _matmul_wt(a, wt, row_of, n, out_dtype, tm, tn, name, transpose_out=False, scale_of=None):
    m, k = a.shape
    if transpose_out:
        out_spec = pl.BlockSpec((tn, tm), lambda j, i: (j, i))
        out_shape = jax.ShapeDtypeStruct((n, m), out_dtype)
    else:
        out_spec = pl.BlockSpec((tm, tn), lambda j, i: (i, j))
        out_shape = jax.ShapeDtypeStruct((m, n), out_dtype)
    return pl.pallas_call(
        functools.partial(_mm_wt_kernel, row_of=row_of, tn=tn, transpose_out=transpose_out,
                          scale_of=scale_of),
        grid=(n // tn, m // tm),
        in_specs=[pl.BlockSpec((tm, k), lambda j, i: (i, 0)),
                  pl.BlockSpec(memory_space=pl.ANY)],
        out_specs=out_spec,
        out_shape=out_shape,
        scratch_shapes=[pltpu.VMEM((tn, k), F32), pltpu.VMEM((tn, k), BF16),
                        pltpu.SemaphoreType.DMA((1,))],
        compiler_params=_params("arbitrary", "arbitrary"),
        name=name,
    )(a, wt)


def _mm_res_kernel(a_ref, w_hbm, r_ref, o_ref, stage_ref, wq_ref, sem, *, k0, tn):
    def tile_copies(jj):
        rows = pl.ds(k0, stage_ref.shape[0])
        cols = pl.ds(pl.multiple_of(jj * tn, tn), tn)
        return [pltpu.make_async_copy(w_hbm.at[rows, cols], stage_ref, sem.at[0])]

    _stage_weight_tiles(tile_copies, [stage_ref], [wq_ref])
    o_ref[...] = r_ref[...] + jnp.dot(a_ref[...], wq_ref[...], preferred_element_type=F32)


def _matmul_residual(a, w, r, tm, tn, name, k_slab=0, n_slabs=1):
    m = a.shape[0]
    k, n = w.shape
    kk = k // n_slabs
    return pl.pallas_call(
        functools.partial(_mm_res_kernel, k0=k_slab * kk, tn=tn),
        grid=(n // tn, m // tm),
        in_specs=[pl.BlockSpec((tm, kk), lambda j, i: (i, k_slab)),
                  pl.BlockSpec(memory_space=pl.ANY),
                  pl.BlockSpec((tm, tn), lambda j, i: (i, j))],
        out_specs=pl.BlockSpec((tm, tn), lambda j, i: (i, j)),
        out_shape=jax.ShapeDtypeStruct((m, n), F32),
        scratch_shapes=[pltpu.VMEM((kk, tn), F32), pltpu.VMEM((kk, tn), BF16),
                        pltpu.SemaphoreType.DMA((1,))],
        compiler_params=_params("arbitrary", "arbitrary"),
        name=name,
    )(a, w, r)


def _merge_kernel(oa_ref, ob_ref, wa_ref, wb_ref, ga_ref, gb_ref, ba_ref, bb_ref, o_ref,
                  waq_ref, wbq_ref):
    _cast_weights([(wa_ref, waq_ref), (wb_ref, wbq_ref)])
    ua = jnp.dot(oa_ref[...], waq_ref[...], preferred_element_type=F32)
    ub = jnp.dot(ob_ref[...], wbq_ref[...], preferred_element_type=F32)
    merged = (jax.nn.sigmoid(ga_ref[...] + ba_ref[...]) * ua
              + jax.nn.sigmoid(gb_ref[...] + bb_ref[...]) * ub)
    o_ref[...] = merged.astype(o_ref.dtype)


def _merge(o_a, o_b, w_a, w_b, gates, b_gate, tm=1024, tn=512):
    m, k = o_a.shape
    d = w_a.shape[1]
    nb = d // tn
    bg = b_gate.reshape(1, 2 * d)
    return pl.pallas_call(
        _merge_kernel,
        grid=(nb, m // tm),
        in_specs=[pl.BlockSpec((tm, k), lambda j, i: (i, 0)),
                  pl.BlockSpec((tm, k), lambda j, i: (i, 0)),
                  pl.BlockSpec((k, tn), lambda j, i: (0, j)),
                  pl.BlockSpec((k, tn), lambda j, i: (0, j)),
                  pl.BlockSpec((tm, tn), lambda j, i: (i, j)),
                  pl.BlockSpec((tm, tn), lambda j, i: (i, j + nb)),
                  pl.BlockSpec((1, tn), lambda j, i: (0, j)),
                  pl.BlockSpec((1, tn), lambda j, i: (0, j + nb))],
        out_specs=pl.BlockSpec((tm, tn), lambda j, i: (i, j)),
        out_shape=jax.ShapeDtypeStruct((m, d), BF16),
        scratch_shapes=[pltpu.VMEM((k, tn), BF16), pltpu.VMEM((k, tn), BF16)],
        compiler_params=_params("arbitrary", "arbitrary"),
        name="merge",
    )(o_a, o_b, w_a, w_b, gates, gates, bg, bg)


def _swiglu_up_kernel(a_ref, wg_hbm, wu_hbm, o_ref, sg_ref, su_ref, wgq_ref, wuq_ref, sem, *, tn):
    def tile_copies(jj):
        cols = pl.ds(pl.multiple_of(jj * tn, tn), tn)
        return [pltpu.make_async_copy(wg_hbm.at[:, cols], sg_ref, sem.at[0]),
                pltpu.make_async_copy(wu_hbm.at[:, cols], su_ref, sem.at[1])]

    _stage_weight_tiles(tile_copies, [sg_ref, su_ref], [wgq_ref, wuq_ref])
    a = a_ref[...]
    g = jnp.dot(a, wgq_ref[...], preferred_element_type=F32)
    u = jnp.dot(a, wuq_ref[...], preferred_element_type=F32)
    o_ref[...] = (g * jax.nn.sigmoid(g) * u).astype(o_ref.dtype)


def _swiglu_up(a, wg, wu, tm=2048, tn=256):
    m, k = a.shape
    f = wg.shape[1]
    return pl.pallas_call(
        functools.partial(_swiglu_up_kernel, tn=tn),
        grid=(f // tn, m // tm),
        in_specs=[pl.BlockSpec((tm, k), lambda j, i: (i, 0)),
                  pl.BlockSpec(memory_space=pl.ANY),
                  pl.BlockSpec(memory_space=pl.ANY)],
        out_specs=pl.BlockSpec((tm, tn), lambda j, i: (i, j)),
        out_shape=jax.ShapeDtypeStruct((m, f), BF16),
        scratch_shapes=[pltpu.VMEM((k, tn), F32), pltpu.VMEM((k, tn), F32),
                        pltpu.VMEM((k, tn), BF16), pltpu.VMEM((k, tn), BF16),
                        pltpu.SemaphoreType.DMA((2,))],
        compiler_params=_params("arbitrary", "arbitrary"),
        name="swiglu_up",
    )(a, wg, wu)


def _cum_kernel(f_ref, b_ref, c_ref, ct_ref):
    x = f_ref[...] + b_ref[...]
    c = jnp.minimum(x, 0.0) - jnp.log1p(jnp.exp(-jnp.abs(x)))
    n = c.shape[0]
    row = lax.broadcasted_iota(jnp.int32, c.shape, 0)
    shift = 1
    while shift < n:
        c = c + jnp.where(row >= shift, pltpu.roll(c, shift, axis=0), 0.0)
        shift *= 2
    c = c * LOG2E
    c_ref[...] = c
    ct_ref[...] = c.T


def _forget_scan(f_pad, b_pad, batch, seq):
    return pl.pallas_call(
        _cum_kernel,
        grid=(batch,),
        in_specs=[pl.BlockSpec((seq, LANES), lambda b: (b, 0)),
                  pl.BlockSpec((1, LANES), lambda b: (0, 0))],
        out_specs=[pl.BlockSpec((seq, LANES), lambda b: (b, 0)),
                   pl.BlockSpec((None, LANES, seq), lambda b: (b, 0, 0))],
        out_shape=[jax.ShapeDtypeStruct((batch * seq, LANES), F32),
                   jax.ShapeDtypeStruct((batch, LANES, seq), F32)],
        compiler_params=_params("parallel"),
        name="forget_scan",
    )(f_pad, b_pad)


N_SPLIT = 3


def _bias_split_kernel(c_ref, o_ref):
    c = c_ref[...]
    lane = lax.broadcasted_iota(jnp.int32, c.shape, 1)
    for hh in range(N_HEADS):
        rest = jnp.broadcast_to(c[:, hh:hh + 1], c.shape)
        out = jnp.zeros(c.shape, F32)
        for piece in range(N_SPLIT):
            part = rest.astype(BF16).astype(F32)
            out = jnp.where(lane == piece, part, out)
            rest = rest - part
        o_ref[hh] = out.astype(BF16)


def _forget_bias_split(c, batch, seq, tb=512):
    nb = seq // tb
    return pl.pallas_call(
        _bias_split_kernel,
        grid=(batch, nb),
        in_specs=[pl.BlockSpec((tb, LANES), lambda b, t: (b * nb + t, 0))],
        out_specs=pl.BlockSpec((N_HEADS, tb, LANES), lambda b, t: (b, t, 0)),
        out_shape=jax.ShapeDtypeStruct((batch * N_HEADS, seq, LANES), BF16),
        compiler_params=_params("parallel", "parallel"),
        name="forget_bias_split",
    )(c)


def _head_rows(u):
    return slice(u * HEAD_DIM, (u + 1) * HEAD_DIM)


def _exp2_and_sum(a, shift):
    p = jnp.exp2(a + shift)
    return p.astype(BF16), jnp.sum(p, axis=0, keepdims=True)


def _attn_a_kernel(e_ref, qt_ref, k_ref, vt_ref, o_ref, bias_ref):
    qi = pl.program_id(2)

    @pl.when(qi == 0)
    def _():
        qc = lax.broadcasted_iota(jnp.int32, (TQ_A, KB_A), 0) // CHUNK
        kc = lax.broadcasted_iota(jnp.int32, (TQ_A, KB_A), 1) // CHUNK
        valid = (kc >= qc) & (kc <= qc + LEFT_CHUNKS)
        for u in range(HEADS_A):
            e = jnp.broadcast_to(e_ref[u], (TQ_A, E_LEN))
            t = pltpu.roll(e, 0, 1, stride=1, stride_axis=0)[:, E_LEN - KB_A:]
            bias_ref[u] = (jnp.where(valid, t, NEG_INF) * LOG2E).T

    starts = [pl.multiple_of(jnp.maximum(qi - back, 0) * TQ_A, TQ_A) for back in (2, 1, 0)]
    pens = [jnp.where(qi >= 2, 0.0, NEG_INF), jnp.where(qi >= 1, 0.0, NEG_INF), None]
    heads = [_head_rows(u) for u in range(HEADS_A)]
    scores = []
    for u, hd in enumerate(heads):
        qt = qt_ref[hd, :]
        parts = []
        for blk in range(3):
            k = k_ref[pl.ds(starts[blk], TQ_A), hd]
            s = jnp.dot(k, qt, preferred_element_type=F32)
            s = s + bias_ref[u, blk * TQ_A:(blk + 1) * TQ_A, :]
            if pens[blk] is not None:
                s = s + pens[blk]
            parts.append(s)
        scores.append(jnp.concatenate(parts, axis=0))
    probs = []
    for s in scores:
        probs.append(_exp2_and_sum(s, -jnp.max(s, axis=0, keepdims=True)))
    for hd, (p, l) in zip(heads, probs):
        vt = jnp.concatenate([vt_ref[hd, pl.ds(st, TQ_A)] for st in starts], axis=1)
        ot = jnp.dot(vt, p, preferred_element_type=F32)
        o_ref[:, hd] = (ot / l).T.astype(o_ref.dtype)


def _attn_a(qvt, kk, e_tab, batch, seq):
    nq = seq // TQ_A
    hb = HEADS_A
    w = hb * HEAD_DIM
    ng = N_HEADS // hb
    return pl.pallas_call(
        _attn_a_kernel,
        grid=(batch, ng, nq),
        in_specs=[pl.BlockSpec((hb, 1, E_LEN), lambda b, g, qi: (g, 0, 0)),
                  pl.BlockSpec((w, TQ_A), lambda b, g, qi: (g, b * nq + qi)),
                  pl.BlockSpec((seq, w), lambda b, g, qi: (b, g)),
                  pl.BlockSpec((w, seq), lambda b, g, qi: (ng + g, b))],
        out_specs=pl.BlockSpec((TQ_A, w), lambda b, g, qi: (b * nq + qi, g)),
        out_shape=jax.ShapeDtypeStruct((batch * seq, WIDTH), BF16),
        scratch_shapes=[pltpu.VMEM((hb, KB_A, TQ_A), F32)],
        compiler_params=_params("arbitrary", "arbitrary", "arbitrary"),
        name="attn_a",
    )(e_tab, qvt, kk, qvt)


def _fox_kernel(qt_ref, k_ref, vt_ref, cs_ref, ct_ref, o_ref):
    qi = pl.program_id(2)
    hb = HEADS_B
    q0 = pl.multiple_of(qi * T_B, T_B)
    row = lax.broadcasted_iota(jnp.int32, (LANES, T_B), 0)
    minus_ones = jnp.where(row < N_SPLIT, -1.0, 0.0).astype(BF16)

    def step(j, carry, masked):
        start = pl.multiple_of(j * T_B, T_B)
        heads = [_head_rows(u) for u in range(hb)]
        logits = []
        for u, hd in enumerate(heads):
            k_aug = jnp.concatenate([k_ref[pl.ds(start, T_B), hd],
                                     cs_ref[u, pl.ds(start, T_B), :]], axis=1)
            qt_aug = jnp.concatenate([qt_ref[hd, :], minus_ones], axis=0)
            a = jnp.dot(k_aug, qt_aug, preferred_element_type=F32)
            if masked:
                r = lax.broadcasted_iota(jnp.int32, a.shape, 0)
                c = lax.broadcasted_iota(jnp.int32, a.shape, 1)
                a = jnp.where(r <= c, a, NEG_INF)
            logits.append(a)
        stats = []
        for u, a in enumerate(logits):
            m, l, _ = carry[u]
            cq = ct_ref[u, :, pl.ds(q0, T_B)]
            m_new = jnp.maximum(m, jnp.max(a, axis=0, keepdims=True) + cq)
            alpha = jnp.exp2(m - m_new)
            p, p_sum = _exp2_and_sum(a, cq - m_new)
            stats.append((m_new, alpha * l + p_sum, alpha, p))
        out = []
        for u, hd in enumerate(heads):
            m_new, l, alpha, p = stats[u]
            pv = jnp.dot(vt_ref[hd, pl.ds(start, T_B)], p,
                         preferred_element_type=F32)
            out.append((m_new, l, alpha * carry[u][2] + pv))
        return tuple(out)

    init = tuple((jnp.full((1, T_B), NEG_INF, F32), jnp.zeros((1, T_B), F32),
                  jnp.zeros((HEAD_DIM, T_B), F32)) for _ in range(hb))
    carry = lax.fori_loop(0, qi, functools.partial(step, masked=False), init)
    carry = step(qi, carry, masked=True)
    for u in range(hb):
        _, l, acc = carry[u]
        o_ref[:, _head_rows(u)] = (acc / l).T.astype(o_ref.dtype)


def _fox(qvt, kk, cs, ct, batch, seq):
    nq = seq // T_B
    hb = HEADS_B
    w = hb * HEAD_DIM
    ng = N_HEADS // hb
    return pl.pallas_call(
        _fox_kernel,
        grid=(batch, ng, nq),
        in_specs=[pl.BlockSpec((w, T_B), lambda b, g, qi: (2 * ng + g, b * nq + qi)),
                  pl.BlockSpec((seq, w), lambda b, g, qi: (b, ng + g)),
                  pl.BlockSpec((w, seq), lambda b, g, qi: (3 * ng + g, b)),
                  pl.BlockSpec((hb, seq, LANES), lambda b, g, qi: (b * ng + g, 0, 0)),
                  pl.BlockSpec((hb, 1, seq), lambda b, g, qi: (b * ng + g, 0, 0))],
        out_specs=pl.BlockSpec((T_B, w), lambda b, g, qi: (b * nq + qi, g)),
        out_shape=jax.ShapeDtypeStruct((batch * seq, WIDTH), BF16),
        compiler_params=_params("parallel", "parallel", "parallel"),
        name="fox",
    )(qvt, kk, qvt, cs, ct)


def _relpos_offset_vector(rel_bias):
    far = jnp.broadcast_to(rel_bias[:, 2 * MAX_REL:], (rel_bias.shape[0], E_LEN - 2 * MAX_REL))
    near = rel_bias[:, :0:-1]
    return jnp.concatenate([far, near], axis=1).astype(F32)[:, None, :]


def kernel(x, g_mix, w_in, b_f, b_gate, rel_bias, w_branch_a, w_branch_b, w_out,
           g_ffn, w_gate_ffn, w_up_ffn, w_down_ffn, g_final):
    batch, seq, d = x.shape
    m = batch * seq
    depth = g_mix.shape[0]
    qkv_w = 6 * WIDTH
    x2 = x.reshape(m, d)
    for l in range(depth):
        w_in_t = jnp.swapaxes(w_in[l], 0, 1)
        b_f_pad = jnp.pad(b_f[l], (0, LANES - N_HEADS)).reshape(1, LANES)
        tn = PROJ_TN
        per = WIDTH // tn

        h = _rmsnorm(x2, g_mix[l], BF16)
        kk = _matmul_wt(h, w_in_t, lambda j: (j + jnp.where(j < per, per, 3 * per)) * tn,
                        2 * WIDTH, BF16, 1024, tn, "proj_k")
        qvt = _matmul_wt(h, w_in_t,
                         lambda j: (j + jnp.where(j < per, 0, jnp.where(j < 3 * per, per, 2 * per))) * tn,
                         4 * WIDTH, BF16, 1024, tn, "proj_qv", transpose_out=True,
                         scale_of=lambda j: jnp.where((j // per) % 2 == 0, SCALE * LOG2E, 1.0))
        f_pad = _matmul_wt(h, w_in_t, lambda j: qkv_w + j * LANES, LANES, F32, 1024, LANES,
                           "proj_forget")
        gates = _matmul_wt(h, w_in_t, lambda j: qkv_w + N_HEADS + j * tn, 2 * d, F32, 1024, tn,
                           "proj_gates")

        c, ct = _forget_scan(f_pad, b_f_pad, batch, seq)
        ct = ct[:, :N_HEADS, :].reshape(batch * N_HEADS, 1, seq)
        cs = _forget_bias_split(c, batch, seq)

        o_a = _attn_a(qvt, kk, _relpos_offset_vector(rel_bias[l]), batch, seq)
        o_b = _fox(qvt, kk, cs, ct, batch, seq)

        merged = _merge(o_a, o_b, w_branch_a[l], w_branch_b[l], gates, b_gate[l])
        x2 = _matmul_residual(merged, w_out[l], x2, 1024, 512, "out_proj")

        h2 = _rmsnorm(x2, g_ffn[l], BF16)
        hidden = _swiglu_up(h2, w_gate_ffn[l], w_up_ffn[l])
        for slab in range(2):
            x2 = _matmul_residual(hidden, w_down_ffn[l], x2, 1024, 512, "ffn_down",
                                  k_slab=slab, n_slabs=2)
    return _rmsnorm(x2, g_final, F32).reshape(batch, seq, d)
```
